```python
import math
import jax, jax.numpy as jnp
from jax import lax
import numpy as np

D_MODEL = 1024
BATCH = 8
SEQ = 2048
DEPTH = 4
DEC_BATCH = 128
DEC_SEQ = 1
PAST_LEN = 2048
PAGE_SIZE = 128

HEAD_DIM = 64
N_MOBA_HEADS = 8
N_SB_HEADS = 8
N_ATT_HEADS = N_MOBA_HEADS + N_SB_HEADS
MOBA_BLOCK = 256
MOBA_TOPK = 3
MOBA_QCHUNK = 16
SB_QBLOCK = 128
N_REC_HEADS = 8
REC_HEAD_DIM = D_MODEL // N_REC_HEADS
REC_CHUNK = 64
D_FF = 4 * D_MODEL
N_ATT_LAYERS = (DEPTH + 1) // 2
N_REC_LAYERS = DEPTH // 2
ATT_IN = 3 * N_ATT_HEADS * HEAD_DIM
REC_IN = 4 * D_MODEL + 2 * N_REC_HEADS
EPS = 1e-6
ALIBI_SLOPES = tuple(2.0 ** (-8.0 * (h + 1) / N_MOBA_HEADS) for h in range(N_MOBA_HEADS))

kernel_name = 'moba_stickbreak_mlstm_hybrid_step'

F32 = jnp.float32


def rmsnorm(x, g):
    xf = x.astype(F32)
    y = xf * lax.rsqrt(jnp.mean(xf * xf, axis=-1, keepdims=True) + EPS)
    return (y * g.astype(F32)).astype(x.dtype)


def moba_attention(q, k, v, pos0):
    B, T, H, d = q.shape
    L = k.shape[1]
    nb = -(-L // MOBA_BLOCK)
    pad = nb * MOBA_BLOCK - L
    if pad:
        k = jnp.pad(k, ((0, 0), (0, pad), (0, 0), (0, 0)))
        v = jnp.pad(v, ((0, 0), (0, pad), (0, 0), (0, 0)))
    kb = k.reshape(B, nb, MOBA_BLOCK, H, d)
    vb = v.reshape(B, nb, MOBA_BLOCK, H, d)
    kmean = jnp.mean(kb.astype(F32), axis=2)
    ksel = min(MOBA_TOPK, nb)
    slopes = jnp.asarray(ALIBI_SLOPES, F32)
    scale = d ** -0.5
    qc = math.gcd(T, MOBA_QCHUNK)
    nc = T // qc
    qs = q.reshape(B, nc, qc, H, d).transpose(1, 0, 2, 3, 4)
    ps = (pos0 + jnp.arange(T, dtype=jnp.int32)).reshape(nc, qc)
    bi = jnp.arange(B)[:, None, None, None]
    hi = jnp.arange(H)[None, :, None, None]
    offs = jnp.arange(MOBA_BLOCK, dtype=jnp.int32)
    blk_ids = jnp.arange(nb, dtype=jnp.int32)

    def one_chunk(args):
        qq, pos = args
        own = pos // MOBA_BLOCK
        gate = jnp.einsum('bthd,bnhd->bhtn', qq.astype(F32), kmean)
        gate = jnp.where(blk_ids[None, :] < own[:, None], gate, -jnp.inf)
        _, sel = lax.top_k(gate, ksel)
        sel_ok = sel < own[:, None]
        idx = jnp.concatenate([sel, jnp.broadcast_to(own[:, None], (B, H, qc, 1))], axis=-1)
        ok = jnp.concatenate([sel_ok, jnp.ones((B, H, qc, 1), bool)], axis=-1)
        kg = kb[bi, idx, :, hi, :]
        vg = vb[bi, idx, :, hi, :]
        kpos = idx[..., None] * MOBA_BLOCK + offs
        dist = (pos[:, None, None] - kpos).astype(F32)
        s = jnp.einsum('bthd,bhtsjd->bhtsj', qq, kg).astype(F32) * scale - slopes[:, None, None, None] * dist
        mask = ok[..., None] & (kpos <= pos[:, None, None])
        s = jnp.where(mask, s, -jnp.inf)
        p = jax.nn.softmax(s.reshape(B, H, qc, -1), axis=-1).reshape(s.shape)
        return jnp.einsum('bhtsj,bhtsjd->bthd', p.astype(v.dtype), vg)

    out = lax.map(one_chunk, (qs, ps))
    return out.transpose(1, 0, 2, 3, 4).reshape(B, T, H, d)


def stick_breaking_attention(q, k, v, pos0):
    B, T, H, d = q.shape
    L = k.shape[1]
    scale = d ** -0.5
    qb = math.gcd(T, SB_QBLOCK)
    nq = T // qb
    qs = q.reshape(B, nq, qb, H, d).transpose(1, 0, 2, 3, 4)
    ps = (pos0 + jnp.arange(T, dtype=jnp.int32)).reshape(nq, qb)
    kpos = jnp.arange(L, dtype=jnp.int32)

    def one_block(args):
        qq, pos = args
        z = jnp.einsum('bthd,bshd->bhts', qq, k).astype(F32) * scale
        mask = kpos[None, :] < pos[:, None]
        log_keep = jnp.where(mask, jax.nn.log_sigmoid(-z), 0.0)
        later = lax.cumsum(log_keep, axis=3, reverse=True) - log_keep
        w = jnp.where(mask, jnp.exp(jax.nn.log_sigmoid(z) + later), 0.0)
        return jnp.einsum('bhts,bshd->bthd', w.astype(v.dtype), v)

    out = lax.map(one_block, (qs, ps))
    return out.transpose(1, 0, 2, 3, 4).reshape(B, T, H, d)


def mlstm_chunkwise(q, k, v, i_pre, f_pre, C0, n0, m0):
    B, T, H, dk = q.shape
    dv = v.shape[-1]
    lc = math.gcd(T, REC_CHUNK)
    nc = T // lc

    def to_chunks(a):
        a = a.astype(F32).reshape((B, nc, lc, H) + a.shape[3:])
        return jnp.moveaxis(jnp.moveaxis(a, 1, 0), 3, 2)

    qs = to_chunks(q)
    ks = to_chunks(k) * (dk ** -0.5)
    vs = to_chunks(v)
    is_ = to_chunks(i_pre)
    lfs = to_chunks(jax.nn.log_sigmoid(f_pre.astype(F32)))
    tril = jnp.tril(jnp.ones((lc, lc), bool))

    def step(carry, xs):
        C, n, m = carry
        qc, kc, vc, ic, lf = xs
        b = jnp.cumsum(lf, axis=-1)
        a = ic - b
        mt = b + jnp.maximum(m[..., None], lax.cummax(a, axis=a.ndim - 1))
        inter = jnp.exp(m[..., None] + b - mt)
        logD = jnp.where(tril, a[..., None, :] + (b - mt)[..., :, None], -jnp.inf)
        S = jnp.exp(logD) * jnp.einsum('bhtd,bhsd->bhts', qc, kc)
        num = inter[..., None] * jnp.einsum('bhtd,bhde->bhte', qc, C) + jnp.einsum('bhts,bhse->bhte', S, vc)
        den = inter * jnp.einsum('bhtd,bhd->bht', qc, n) + jnp.sum(S, axis=-1)
        h = num / jnp.maximum(jnp.abs(den), jnp.exp(-mt))[..., None]
        m_new = mt[..., -1]
        w_last = jnp.exp(a + (b[..., -1] - m_new)[..., None])
        decay = jnp.exp(m + b[..., -1] - m_new)
        C_new = decay[..., None, None] * C + jnp.einsum('bhs,bhsd,bhse->bhde', w_last, kc, vc)
        n_new = decay[..., None] * n + jnp.einsum('bhs,bhsd->bhd', w_last, kc)
        return (C_new, n_new, m_new), h

    (C, n, m), hs = lax.scan(step, (C0.astype(F32), n0.astype(F32), m0.astype(F32)), (qs, ks, vs, is_, lfs))
    h = jnp.swapaxes(jnp.moveaxis(hs, 0, 1), 2, 3).reshape(B, T, H, dv)
    return h, C, n, m


def att_mixer(h, pos0, past_k, past_v, w_in, qn_g, kn_g, w_out):
    B, T, _ = h.shape
    nm = N_MOBA_HEADS
    proj = (h @ w_in).reshape(B, T, 3, N_ATT_HEADS, HEAD_DIM)
    q, k, v = proj[:, :, 0], proj[:, :, 1], proj[:, :, 2]
    q = jnp.concatenate([rmsnorm(q[:, :, :nm], qn_g), q[:, :, nm:]], axis=2)
    k = jnp.concatenate([rmsnorm(k[:, :, :nm], kn_g), k[:, :, nm:]], axis=2)
    if past_k is None:
        k_all, v_all = k, v
    else:
        k_all = jnp.concatenate([past_k, k], axis=1)
        v_all = jnp.concatenate([past_v, v], axis=1)
    oa = moba_attention(q[:, :, :nm], k_all[:, :, :nm], v_all[:, :, :nm], pos0)
    ob = stick_breaking_attention(q[:, :, nm:], k_all[:, :, nm:], v_all[:, :, nm:], pos0)
    o = jnp.concatenate([oa, ob], axis=2).reshape(B, T, N_ATT_HEADS * HEAD_DIM)
    return o @ w_out, k, v


def rec_mixer(h, C0, n0, m0, w_in, b_i, b_f, hn_g, w_out):
    B, T, D = h.shape
    H, dh = N_REC_HEADS, REC_HEAD_DIM
    proj = h @ w_in
    q = proj[..., :D].reshape(B, T, H, dh)
    k = proj[..., D:2 * D].reshape(B, T, H, dh)
    v = proj[..., 2 * D:3 * D].reshape(B, T, H, dh)
    o = proj[..., 3 * D:4 * D]
    ig = proj[..., 4 * D:4 * D + H] + b_i
    fg = proj[..., 4 * D + H:] + b_f
    hh, C, n, m = mlstm_chunkwise(q, k, v, ig, fg, C0, n0, m0)
    hh = rmsnorm(hh, hn_g).reshape(B, T, D).astype(h.dtype) * jax.nn.sigmoid(o)
    return hh @ w_out, C, n, m


def decoder(x, c, cache_k, cache_v, page_table, state_C, state_n, state_m,
            ada_w, ada_b, norm1_g, norm2_g, att_w_in, att_qn_g, att_kn_g, att_w_out,
            rec_w_in, rec_b_i, rec_b_f, rec_hn_g, rec_w_out, mlp_w1, mlp_w2):
    B, T, D = x.shape
    pos0 = 0 if cache_k is None else page_table.shape[1] * cache_k.shape[2]
    sc_in = jax.nn.silu(c)
    ks, vs, Cs, ns, ms = [], [], [], [], []
    for l in range(DEPTH):
        mod = sc_in @ ada_w[l] + ada_b[l]
        sh1, sc1, g1, sh2, sc2, g2 = jnp.split(mod[:, None, :], 6, axis=-1)
        h = rmsnorm(x, norm1_g[l]) * (1.0 + sc1) + sh1
        j = l // 2
        if l % 2 == 0:
            if cache_k is None:
                pk = pv = None
            else:
                pk = cache_k[j, page_table].reshape(B, -1, N_ATT_HEADS, HEAD_DIM)
                pv = cache_v[j, page_table].reshape(B, -1, N_ATT_HEADS, HEAD_DIM)
            y, kn, vn = att_mixer(h, pos0, pk, pv, att_w_in[j], att_qn_g[j], att_kn_g[j], att_w_out[j])
            ks.append(kn)
            vs.append(vn)
        else:
            if state_C is None:
                C0 = jnp.zeros((B, N_REC_HEADS, REC_HEAD_DIM, REC_HEAD_DIM), F32)
                n0 = jnp.zeros((B, N_REC_HEADS, REC_HEAD_DIM), F32)
                m0 = jnp.zeros((B, N_REC_HEADS), F32)
            else:
                C0, n0, m0 = state_C[j], state_n[j], state_m[j]
            y, Cn, nn_, mn = rec_mixer(h, C0, n0, m0, rec_w_in[j], rec_b_i[j], rec_b_f[j], rec_hn_g[j], rec_w_out[j])
            Cs.append(Cn)
            ns.append(nn_)
            ms.append(mn)
        x = x + g1 * y
        h = rmsnorm(x, norm2_g[l]) * (1.0 + sc2) + sh2
        x = x + g2 * (jnp.square(jax.nn.relu(h @ mlp_w1[l])) @ mlp_w2[l])
    return x, jnp.stack(ks), jnp.stack(vs), jnp.stack(Cs), jnp.stack(ns), jnp.stack(ms)


def setup_inputs(seed: int = 0) -> dict:
    key = jax.random.key(seed)
    ks = jax.random.split(key, 32)
    n_pages = PAST_LEN // PAGE_SIZE
    used = DEC_BATCH * n_pages
    n_pool = used + used // 4
    nrm = jax.random.normal
    perm = jax.random.permutation(ks[5], n_pool)[:used]
    return {
        'x_prompt': nrm(ks[0], (BATCH, SEQ, D_MODEL), F32),
        'x_sample': nrm(ks[1], (DEC_BATCH, DEC_SEQ, D_MODEL), F32),
        'cache_k': nrm(ks[2], (N_ATT_LAYERS, n_pool, PAGE_SIZE, N_ATT_HEADS, HEAD_DIM), F32),
        'cache_v': nrm(ks[3], (N_ATT_LAYERS, n_pool, PAGE_SIZE, N_ATT_HEADS, HEAD_DIM), F32),
        'page_table': perm.reshape(DEC_BATCH, n_pages).astype(jnp.int32),
        'state_C': 0.3 * nrm(ks[6], (N_REC_LAYERS, DEC_BATCH, N_REC_HEADS, REC_HEAD_DIM, REC_HEAD_DIM), F32),
        'state_n': 0.3 * nrm(ks[7], (N_REC_LAYERS, DEC_BATCH, N_REC_HEADS, REC_HEAD_DIM), F32),
        'state_m': nrm(ks[8], (N_REC_LAYERS, DEC_BATCH, N_REC_HEADS), F32),
        'c_prompt': nrm(ks[9], (BATCH, D_MODEL), F32),
        'c_sample': nrm(ks[10], (DEC_BATCH, D_MODEL), F32),
        'ada_w': 0.3 * D_MODEL ** -0.5 * nrm(ks[11], (DEPTH, D_MODEL, 6 * D_MODEL), F32),
        'ada_b': 0.1 * nrm(ks[12], (DEPTH, 6 * D_MODEL), F32),
        'norm1_g': 1.0 + 0.1 * nrm(ks[13], (DEPTH, D_MODEL), F32),
        'norm2_g': 1.0 + 0.1 * nrm(ks[14], (DEPTH, D_MODEL), F32),
        'att_w_in': D_MODEL ** -0.5 * nrm(ks[15], (N_ATT_LAYERS, D_MODEL, ATT_IN), F32),
        'att_qn_g': 1.0 + 0.1 * nrm(ks[16], (N_ATT_LAYERS, HEAD_DIM), F32),
        'att_kn_g': 1.0 + 0.1 * nrm(ks[17], (N_ATT_LAYERS, HEAD_DIM), F32),
        'att_w_out': (N_ATT_HEADS * HEAD_DIM) ** -0.5 * nrm(ks[18], (N_ATT_LAYERS, N_ATT_HEADS * HEAD_DIM, D_MODEL), F32),
        'rec_w_in': D_MODEL ** -0.5 * nrm(ks[19], (N_REC_LAYERS, D_MODEL, REC_IN), F32),
        'rec_b_i': 0.1 * nrm(ks[20], (N_REC_LAYERS, N_REC_HEADS), F32),
        'rec_b_f': 3.0 + 0.5 * nrm(ks[21], (N_REC_LAYERS, N_REC_HEADS), F32),
        'rec_hn_g': 1.0 + 0.1 * nrm(ks[22], (N_REC_LAYERS, N_REC_HEADS, REC_HEAD_DIM), F32),
        'rec_w_out': D_MODEL ** -0.5 * nrm(ks[23], (N_REC_LAYERS, D_MODEL, D_MODEL), F32),
        'mlp_w1': D_MODEL ** -0.5 * nrm(ks[24], (DEPTH, D_MODEL, D_FF), F32),
        'mlp_w2': D_FF ** -0.5 * nrm(ks[25], (DEPTH, D_FF, D_MODEL), F32),
    }


def reference(x_prompt, x_sample, cache_k, cache_v, page_table, state_C, state_n, state_m, c_prompt, c_sample,
              ada_w, ada_b, norm1_g, norm2_g, att_w_in, att_qn_g, att_kn_g, att_w_out,
              rec_w_in, rec_b_i, rec_b_f, rec_hn_g, rec_w_out, mlp_w1, mlp_w2):
    y_prompt, k_prompt, v_prompt, C_prompt, n_prompt, m_prompt = decoder(
        x_prompt, c_prompt, None, None, None, None, None, None,
        ada_w, ada_b, norm1_g, norm2_g, att_w_in, att_qn_g, att_kn_g, att_w_out,
        rec_w_in, rec_b_i, rec_b_f, rec_hn_g, rec_w_out, mlp_w1, mlp_w2)
    y_sample, k_sample, v_sample, C_sample, n_sample, m_sample = decoder(
        x_sample, c_sample, cache_k, cache_v, page_table, state_C, state_n, state_m,
        ada_w, ada_b, norm1_g, norm2_g, att_w_in, att_qn_g, att_kn_g, att_w_out,
        rec_w_in, rec_b_i, rec_b_f, rec_hn_g, rec_w_out, mlp_w1, mlp_w2)
    return (y_prompt, y_sample, k_prompt, v_prompt, k_sample, v_sample,
            C_prompt, n_prompt, m_prompt, C_sample, n_sample, m_sample)
```

```python
import functools

import jax
import jax.numpy as jnp
from jax import lax
from jax.experimental import pallas as pl
from jax.experimental.pallas import tpu as pltpu

F32 = jnp.float32
BF16 = jnp.bfloat16

EPS = 1e-6
HEAD_DIM = 64
N_MOBA_HEADS = 8
N_SB_HEADS = 8
N_ATT_HEADS = N_MOBA_HEADS + N_SB_HEADS
MOBA_BLOCK = 256
MOBA_TOPK = 3
N_REC_HEADS = 8
REC_CHUNK = 64
LANES = 128
NEG = -1e30
VMEM_LIMIT = 56 * 1024 * 1024
MOBA_COLS = N_MOBA_HEADS * HEAD_DIM


def _cp(*sem):
    return pltpu.CompilerParams(dimension_semantics=sem, vmem_limit_bytes=VMEM_LIMIT)


def _sigmoid(x):
    return 1.0 / (1.0 + jnp.exp(-x))


def _softplus_tail(z):
    return jnp.log(1.0 + jnp.exp(-jnp.abs(z)))


def _log_sigmoid(z):
    return jnp.minimum(z, 0.0) - _softplus_tail(z)


def _dot(a, b):
    return jnp.dot(a, b, preferred_element_type=F32)


def _dot_nt(a, b):
    return lax.dot_general(a, b, (((1,), (1,)), ((), ())), preferred_element_type=F32)


def _dot_tn(a, b):
    return lax.dot_general(a, b, (((0,), (0,)), ((), ())), preferred_element_type=F32)


def _hilo(x):
    hi = x.astype(BF16)
    lo = (x - hi.astype(F32)).astype(BF16)
    return jnp.concatenate([hi, lo], axis=-1)


def _iota(shape, dim):
    return lax.broadcasted_iota(jnp.int32, shape, dim)


def _alibi_slopes():
    return [2.0 ** (-8.0 * (h + 1) / N_MOBA_HEADS) for h in range(N_MOBA_HEADS)]


def _mod_kernel(c_ref, w_ref, b_ref, o_ref):
    c = c_ref[...]
    s = c * _sigmoid(c)
    o_ref[...] = _dot(s.astype(BF16), w_ref[...].astype(BF16)) + b_ref[...]


def _adaln_mod(c_all, ada_w, ada_b):
    depth, d, n = ada_w.shape
    m = c_all.shape[0]
    tn = 1024
    return pl.pallas_call(
        _mod_kernel,
        grid=(depth, n // tn),
        in_specs=[
            pl.BlockSpec((m, d), lambda l, j: (0, 0)),
            pl.BlockSpec((None, d, tn), lambda l, j: (l, 0, j)),
            pl.BlockSpec((None, 1, tn), lambda l, j: (l, 0, j)),
        ],
        out_specs=pl.BlockSpec((None, m, tn), lambda l, j: (l, 0, j)),
        out_shape=jax.ShapeDtypeStruct((depth, m, n), F32),
        compiler_params=_cp("arbitrary", "arbitrary"),
        name="adaln_mod",
    )(c_all, ada_w, ada_b.reshape(depth, 1, n))


class _Mod:
    def __init__(self, arr, l, d, tm, rows_per_batch):
        self.arr, self.l, self.d, self.tm = arr, l, d, tm
        self.per_row = rows_per_batch == 1
        self.blocks_per_batch = None if self.per_row else rows_per_batch // tm

    def spec(self, chunk, tn=None, with_j=False):
        d = self.d
        tn = d if tn is None else tn
        l, per = self.l, d // tn
        if self.per_row:
            if with_j:
                return pl.BlockSpec((None, self.tm, tn), lambda i, j, *_: (l, i, chunk * per + j))
            return pl.BlockSpec((None, self.tm, tn), lambda i, *_: (l, i, chunk))
        bpb = self.blocks_per_batch
        if with_j:
            return pl.BlockSpec((None, None, 1, tn), lambda i, j, *_: (l, i // bpb, 0, chunk * per + j))
        return pl.BlockSpec((None, None, 1, tn), lambda i, *_: (l, i // bpb, 0, chunk))


def _norm_mod(x_ref, g_ref, sc_ref, sh_ref):
    x = x_ref[...]
    y = x * lax.rsqrt(jnp.mean(x * x, axis=-1, keepdims=True) + EPS) * g_ref[...]
    return (y * (1.0 + sc_ref[...]) + sh_ref[...]).astype(BF16)


def _nm_mm_kernel(x_ref, g_ref, sc_ref, sh_ref, w_ref, o_ref, h_ref, *, relu2):
    @pl.when(pl.program_id(1) == 0)
    def _():
        h_ref[...] = _norm_mod(x_ref, g_ref, sc_ref, sh_ref)

    acc = _dot(h_ref[...], w_ref[...])
    if relu2:
        acc = jnp.square(jnp.maximum(acc, 0.0))
    o_ref[...] = acc.astype(o_ref.dtype)


def _nm_matmul(x, g_all, mod, chunks, w_all, l_w, *, tn, relu2, out_dtype):
    m, d = x.shape
    n = w_all.shape[-1]
    tm, l = mod.tm, mod.l
    return pl.pallas_call(
        functools.partial(_nm_mm_kernel, relu2=relu2),
        grid=(m // tm, n // tn),
        in_specs=[
            pl.BlockSpec((tm, d), lambda i, j: (i, 0)),
            pl.BlockSpec((None, 1, d), lambda i, j: (l, 0, 0)),
            mod.spec(chunks[0]),
            mod.spec(chunks[1]),
            pl.BlockSpec((None, d, tn), lambda i, j: (l_w, 0, j)),
        ],
        out_specs=pl.BlockSpec((tm, tn), lambda i, j: (i, j)),
        out_shape=jax.ShapeDtypeStruct((m, n), out_dtype),
        scratch_shapes=[pltpu.VMEM((tm, d), BF16)],
        compiler_params=_cp("arbitrary", "arbitrary"),
        name="nm_matmul",
    )(x, g_all, mod.arr, mod.arr, w_all)


def _rec_in_kernel(x_ref, g_ref, sc_ref, sh_ref, w_ref, wg_ref, bg_ref, o_ref, gate_ref, h_ref):
    @pl.when(pl.program_id(1) == 0)
    def _():
        h_ref[...] = _norm_mod(x_ref, g_ref, sc_ref, sh_ref)
        gate_ref[...] = _dot(h_ref[...], wg_ref[...]) + bg_ref[...]

    o_ref[...] = _dot(h_ref[...], w_ref[...])


def _rec_in_proj(x, g_all, mod, w_all, wg_all, bg_all, j_layer, *, tn):
    m, d = x.shape
    n = w_all.shape[-1]
    tm, l = mod.tm, mod.l
    return pl.pallas_call(
        _rec_in_kernel,
        grid=(m // tm, n // tn),
        in_specs=[
            pl.BlockSpec((tm, d), lambda i, j: (i, 0)),
            pl.BlockSpec((None, 1, d), lambda i, j: (l, 0, 0)),
            mod.spec(1),
            mod.spec(0),
            pl.BlockSpec((None, d, tn), lambda i, j: (j_layer, 0, j)),
            pl.BlockSpec((None, d, LANES), lambda i, j: (j_layer, 0, 0)),
            pl.BlockSpec((None, 1, LANES), lambda i, j: (j_layer, 0, 0)),
        ],
        out_specs=[
            pl.BlockSpec((tm, tn), lambda i, j: (i, j)),
            pl.BlockSpec((tm, LANES), lambda i, j: (i, 0)),
        ],
        out_shape=[jax.ShapeDtypeStruct((m, n), F32), jax.ShapeDtypeStruct((m, LANES), F32)],
        scratch_shapes=[pltpu.VMEM((tm, d), BF16)],
        compiler_params=_cp("arbitrary", "arbitrary"),
        name="rec_in_proj",
    )(x, g_all, mod.arr, mod.arr, w_all, wg_all, bg_all)


def _head_rmsnorm_rows(a, gain, gmat):
    ms = _dot(_hilo(a * a), gmat)
    return a * lax.rsqrt(ms + EPS) * gain


def _write_normed_rows(o_ref, a, gain_ref, gm_ref):
    for c in range(MOBA_COLS // LANES):
        sl = slice(c * LANES, (c + 1) * LANES)
        o_ref[:, sl] = _head_rmsnorm_rows(a[:, sl], gain_ref[...], gm_ref[...])
    o_ref[:, MOBA_COLS:] = a[:, MOBA_COLS:]


def _qkv_kernel(x_ref, g_ref, sc_ref, sh_ref, wq_ref, wkt_ref, wvt_ref, qg_ref, kgc_ref, gm_ref, *rest, rows):
    if rows:
        wk_ref, wv_ref, kg_ref, q_ref, kt_ref, vt_ref, k_ref, v_ref = rest
    else:
        q_ref, kt_ref, vt_ref = rest
    h = _norm_mod(x_ref, g_ref, sc_ref, sh_ref)
    tm = h.shape[0]
    _write_normed_rows(q_ref, _dot(h, wq_ref[...]), qg_ref, gm_ref)
    kt = _dot_nt(wkt_ref[...], h)
    a = kt[:MOBA_COLS].reshape(N_MOBA_HEADS, HEAD_DIM, tm)
    ms = jnp.mean(a * a, axis=1, keepdims=True)
    kt_ref[:MOBA_COLS, :] = (a * lax.rsqrt(ms + EPS) * kgc_ref[...].reshape(1, HEAD_DIM, 1)).reshape(MOBA_COLS, tm)
    kt_ref[MOBA_COLS:, :] = kt[MOBA_COLS:]
    vt_ref[...] = _dot_nt(wvt_ref[...], h)
    if rows:
        _write_normed_rows(k_ref, _dot(h, wk_ref[...]), kg_ref, gm_ref)
        v_ref[...] = _dot(h, wv_ref[...])


def _group_mean_matrix():
    r = jnp.arange(2 * LANES) % LANES
    c = jnp.arange(LANES)
    return jnp.where((r[:, None] // HEAD_DIM) == (c[None, :] // HEAD_DIM), 1.0 / HEAD_DIM, 0.0).astype(BF16)


def _qkv_proj(x, g_all, mod, w, j_layer, rows_per_batch, rows):
    m, d = x.shape
    hd = N_ATT_HEADS * HEAD_DIM
    tm, l = mod.tm, mod.l
    seq = m if rows_per_batch == 1 else rows_per_batch
    batch = m // seq
    bps = seq // tm
    wspec = pl.BlockSpec((None, d, hd), lambda i: (j_layer, 0, 0))
    wtspec = pl.BlockSpec((None, hd, d), lambda i: (j_layer, 0, 0))
    gspec = pl.BlockSpec((None, 1, LANES), lambda i: (j_layer, 0, 0))
    in_specs = [
        pl.BlockSpec((tm, d), lambda i: (i, 0)),
        pl.BlockSpec((None, 1, d), lambda i: (l, 0, 0)),
        mod.spec(1),
        mod.spec(0),
        wspec, wtspec, wtspec, gspec,
        pl.BlockSpec((None, HEAD_DIM, 1), lambda i: (j_layer, 0, 0)),
        pl.BlockSpec((2 * LANES, LANES), lambda i: (0, 0)),
    ]
    args = [x, g_all, mod.arr, mod.arr, w["att_wq"], w["att_wkt"], w["att_wvt"], w["att_qn_g"], w["att_kn_gc"],
            _group_mean_matrix()]
    row_out = jax.ShapeDtypeStruct((m, hd), F32)
    t_out = jax.ShapeDtypeStruct((batch, hd, seq), F32)
    row_spec = pl.BlockSpec((tm, hd), lambda i: (i, 0))
    t_spec = pl.BlockSpec((None, hd, tm), lambda i: (i // bps, 0, i % bps))
    out_shape, out_specs = [row_out, t_out, t_out], [row_spec, t_spec, t_spec]
    if rows:
        in_specs += [wspec, wspec, gspec]
        args += [w["att_wk"], w["att_wv"], w["att_kn_g"]]
        out_shape += [row_out, row_out]
        out_specs += [row_spec, row_spec]
    return pl.pallas_call(
        functools.partial(_qkv_kernel, rows=rows),
        grid=(m // tm,),
        in_specs=in_specs,
        out_specs=out_specs,
        out_shape=out_shape,
        compiler_params=_cp("arbitrary"),
        name="qkv_proj",
    )(*args)


def _res_mm_kernel(a_ref, w_ref, r_ref, gate_ref, o_ref, acc_ref, *, nk):
    k = pl.program_id(2)

    @pl.when(k == 0)
    def _():
        acc_ref[...] = jnp.zeros_like(acc_ref)

    acc_ref[...] += _dot(a_ref[...], w_ref[...])

    @pl.when(k == nk - 1)
    def _():
        o_ref[...] = r_ref[...] + gate_ref[...] * acc_ref[...]


def _res_matmul(a, w_all, l_w, res, mod, gate_chunk, *, tn, tk):
    m, kdim = a.shape
    n = w_all.shape[-1]
    tm = mod.tm
    nk = kdim // tk
    return pl.pallas_call(
        functools.partial(_res_mm_kernel, nk=nk),
        grid=(m // tm, n // tn, nk),
        in_specs=[
            pl.BlockSpec((tm, tk), lambda i, j, k: (i, k)),
            pl.BlockSpec((None, tk, tn), lambda i, j, k: (l_w, k, j)),
            pl.BlockSpec((tm, tn), lambda i, j, k: (i, j)),
            mod.spec(gate_chunk, tn=tn, with_j=True),
        ],
        out_specs=pl.BlockSpec((tm, tn), lambda i, j, k: (i, j)),
        out_shape=jax.ShapeDtypeStruct((m, n), F32),
        scratch_shapes=[pltpu.VMEM((tm, tn), F32)],
        compiler_params=_cp("arbitrary", "arbitrary", "arbitrary"),
        name="res_matmul",
    )(a, w_all, res, mod.arr)


def _moba_prompt_kernel(q_ref, kt_ref, vt_ref, slope_ref, o_ref, *, nb):
    qi = pl.program_id(2)
    blk = MOBA_BLOCK
    scale = HEAD_DIM ** -0.5
    lane = _iota((1, LANES), 1)
    row = _iota((blk, blk), 0)
    col = _iota((blk, blk), 1)
    blk_id = _iota((blk, LANES), 1)
    blk_lane = _iota((LANES, LANES), 1)
    q = q_ref[...]
    kmean_t = jnp.zeros((LANES, LANES), F32)
    for n in range(nb):
        mean_n = jnp.mean(kt_ref[:, n * blk:(n + 1) * blk], axis=-1, keepdims=True)
        kmean_t = jnp.where(blk_lane == n, mean_n, kmean_t)
    outs = []
    for h2 in range(2):
        hmask = (lane < HEAD_DIM) if h2 == 0 else (lane >= HEAD_DIM)
        slope = slope_ref[:, h2 * HEAD_DIM:h2 * HEAD_DIM + 1]
        qf = jnp.where(hmask, q, 0.0)
        qh = qf.astype(BF16)
        gate = jnp.dot(qf, kmean_t, precision=lax.Precision.HIGHEST, preferred_element_type=F32)
        gate = jnp.where(blk_id < qi, gate, -jnp.inf)

        def scores(n):
            ktn = kt_ref[:, pl.ds(pl.multiple_of(n * blk, blk), blk)].astype(BF16)
            dist = ((qi - n) * blk + (row - col)).astype(F32)
            return _dot(qh, ktn) * scale - slope * dist

        def update(carry, s, n):
            m, l, acc = carry
            m_new = jnp.maximum(m, jnp.max(s, axis=-1, keepdims=True))
            alpha = jnp.exp(m - m_new)
            p = jnp.exp(s - m_new)
            vtn = vt_ref[:, pl.ds(pl.multiple_of(n * blk, blk), blk)].astype(BF16)
            return (m_new, alpha * l + jnp.sum(p, axis=-1, keepdims=True),
                    alpha * acc + _dot_nt(p.astype(BF16), vtn))

        def past_block(n, carry):
            g_n = jnp.sum(jnp.where(blk_id == n, gate, 0.0), axis=-1, keepdims=True)
            beats = (gate > g_n) | ((gate == g_n) & (blk_id < n))
            rank = jnp.sum(beats.astype(F32), axis=-1, keepdims=True)
            s = jnp.where(rank < MOBA_TOPK, scores(n), NEG)
            return update(carry, s, n)

        init = (jnp.full((blk, 1), NEG, F32), jnp.zeros((blk, 1), F32), jnp.zeros((blk, LANES), F32))
        carry = lax.fori_loop(0, qi, past_block, init)
        s_own = jnp.where(col <= row, scores(qi), NEG)
        _, l, acc = update(carry, s_own, qi)
        outs.append(acc / l)
    o_ref[...] = jnp.where(lane < HEAD_DIM, outs[0], outs[1]).astype(o_ref.dtype)


def _alibi_table():
    t = jnp.asarray(_alibi_slopes(), F32).reshape(N_MOBA_HEADS // 2, 2, 1)
    return jnp.broadcast_to(t, (N_MOBA_HEADS // 2, 2, HEAD_DIM)).reshape(N_MOBA_HEADS // 2, 1, LANES)


def _moba_prompt(q, kt, vt, batch, seq):
    nb = seq // MOBA_BLOCK
    npair = N_MOBA_HEADS // 2
    return pl.pallas_call(
        functools.partial(_moba_prompt_kernel, nb=nb),
        grid=(batch, npair, nb),
        in_specs=[
            pl.BlockSpec((MOBA_BLOCK, LANES), lambda b, hp, qi: (b * nb + qi, hp)),
            pl.BlockSpec((None, LANES, seq), lambda b, hp, qi: (b, hp, 0)),
            pl.BlockSpec((None, LANES, seq), lambda b, hp, qi: (b, hp, 0)),
            pl.BlockSpec((None, 1, LANES), lambda b, hp, qi: (hp, 0, 0)),
        ],
        out_specs=pl.BlockSpec((MOBA_BLOCK, LANES), lambda b, hp, qi: (b * nb + qi, hp)),
        out_shape=jax.ShapeDtypeStruct((batch * seq, MOBA_COLS), BF16),
        compiler_params=_cp("arbitrary", "arbitrary", "arbitrary"),
        name="moba_prompt",
    )(q, kt, vt, _alibi_table())


SB_BLOCK = 256


def _later_matrix(n):
    j = jnp.arange(2 * n) % n
    s = jnp.arange(n)
    return (j[:, None] > s[None, :]).astype(BF16)


def _sb_prompt_kernel(q_ref, kt_ref, vt_ref, u_ref, o_ref):
    qi = pl.program_id(2)
    blk = SB_BLOCK
    scale = HEAD_DIM ** -0.5
    lane = _iota((1, LANES), 1)
    row = _iota((blk, blk), 0)
    col = _iota((blk, blk), 1)
    q = q_ref[...]
    outs = []
    for h2 in range(2):
        hmask = (lane < HEAD_DIM) if h2 == 0 else (lane >= HEAD_DIM)
        qh = jnp.where(hmask, q, 0.0).astype(BF16)

        def block(n, carry, acc, mask):
            start = pl.multiple_of(n * blk, blk)
            ktn = kt_ref[:, pl.ds(start, blk)].astype(BF16)
            vtn = vt_ref[:, pl.ds(start, blk)].astype(BF16)
            z = _dot(qh, ktn) * scale
            tail = _softplus_tail(z)
            lk = -jnp.maximum(z, 0.0) - tail
            ls = jnp.minimum(z, 0.0) - tail
            if mask is not None:
                lk = jnp.where(mask, lk, 0.0)
            later = _dot(_hilo(lk), u_ref[...]) + carry
            w = jnp.exp(ls + later)
            if mask is not None:
                w = jnp.where(mask, w, 0.0)
            return carry + jnp.sum(lk, axis=-1, keepdims=True), acc + _dot_nt(w.astype(BF16), vtn)

        carry, acc = block(qi, jnp.zeros((blk, 1), F32), jnp.zeros((blk, LANES), F32), col < row)

        def past(r, c):
            return block(qi - 1 - r, c[0], c[1], None)

        _, acc = lax.fori_loop(0, qi, past, (carry, acc))
        outs.append(acc)
    o_ref[...] = jnp.where(lane < HEAD_DIM, outs[0], outs[1]).astype(o_ref.dtype)


def _sb_prompt(q, kt, vt, batch, seq):
    nq = seq // SB_BLOCK
    npair = N_SB_HEADS // 2
    off = N_MOBA_HEADS // 2
    return pl.pallas_call(
        _sb_prompt_kernel,
        grid=(batch, npair, nq),
        in_specs=[
            pl.BlockSpec((SB_BLOCK, LANES), lambda b, hp, qi: (b * nq + qi, off + hp)),
            pl.BlockSpec((None, LANES, seq), lambda b, hp, qi: (b, off + hp, 0)),
            pl.BlockSpec((None, LANES, seq), lambda b, hp, qi: (b, off + hp, 0)),
            pl.BlockSpec((2 * SB_BLOCK, SB_BLOCK), lambda b, hp, qi: (0, 0)),
        ],
        out_specs=pl.BlockSpec((SB_BLOCK, LANES), lambda b, hp, qi: (b * nq + qi, hp)),
        out_shape=jax.ShapeDtypeStruct((batch * seq, N_SB_HEADS * HEAD_DIM), BF16),
        compiler_params=_cp("arbitrary", "arbitrary", "arbitrary"),
        name="sb_prompt",
    )(q, kt, vt, _later_matrix(SB_BLOCK))


def _mlstm_prompt_kernel(p_ref, gate_ref, hn_ref, hh_ref, c_ref, n_ref, m_ref):
    lc = REC_CHUNK
    dh = LANES
    d = N_REC_HEADS * dh

    @pl.when(pl.program_id(1) == 0)
    def _():
        c_ref[...] = jnp.zeros_like(c_ref)
        n_ref[...] = jnp.zeros_like(n_ref)
        m_ref[...] = jnp.zeros_like(m_ref)

    ri = _iota((lc, lc), 0)
    ci = _iota((lc, lc), 1)
    eye = ri == ci
    tril = ci <= ri
    for h in range(N_REC_HEADS):
        sl = slice(h * dh, (h + 1) * dh)
        qc = p_ref[:, sl]
        kc = p_ref[:, d + h * dh:d + (h + 1) * dh] * (dh ** -0.5)
        vc = p_ref[:, 2 * d + h * dh:2 * d + (h + 1) * dh]
        oc = p_ref[:, 3 * d + h * dh:3 * d + (h + 1) * dh]
        i_col = gate_ref[:, h:h + 1]
        lf_col = _log_sigmoid(gate_ref[:, N_REC_HEADS + h:N_REC_HEADS + h + 1])
        c_old = c_ref[h]
        n_old = n_ref[h:h + 1, :]
        m_old = m_ref[h:h + 1, 0:1]
        qb, kb, vb = qc.astype(BF16), kc.astype(BF16), vc.astype(BF16)

        lf_row = jnp.sum(jnp.where(eye, lf_col, 0.0), axis=0, keepdims=True)
        i_row = jnp.sum(jnp.where(eye, i_col, 0.0), axis=0, keepdims=True)
        b_col = jnp.sum(jnp.where(tril, lf_row, 0.0), axis=1, keepdims=True)
        b_row = jnp.sum(jnp.where(ri <= ci, lf_col, 0.0), axis=0, keepdims=True)
        a_row = i_row - b_row
        a_col = i_col - b_col
        cm_col = jnp.max(jnp.where(tril, a_row, -jnp.inf), axis=1, keepdims=True)
        mt_col = b_col + jnp.maximum(m_old, cm_col)
        inter = jnp.exp(m_old + b_col - mt_col)
        dmat = jnp.where(tril, jnp.exp(jnp.where(tril, a_row + (b_col - mt_col), 0.0)), 0.0)
        s = dmat * _dot_nt(qb, kb)
        num = inter * _dot(qb, c_old.astype(BF16)) + _dot(s.astype(BF16), vb)
        den = inter * jnp.sum(qc * n_old, axis=-1, keepdims=True) + jnp.sum(s, axis=-1, keepdims=True)
        hh = num / jnp.maximum(jnp.abs(den), jnp.exp(-mt_col))
        b_last = b_col[lc - 1:lc, :]
        m_new = mt_col[lc - 1:lc, :]
        w_last = jnp.exp(a_col + (b_last - m_new))
        decay = jnp.exp(m_old + b_last - m_new)
        kw = kc * w_last
        c_ref[h] = decay * c_old + _dot_tn(kw.astype(BF16), vb)
        n_ref[h:h + 1, :] = decay * n_old + jnp.sum(kw, axis=0, keepdims=True)
        m_ref[h:h + 1, :] = jnp.broadcast_to(m_new, (1, dh))

        y = hh * lax.rsqrt(jnp.mean(hh * hh, axis=-1, keepdims=True) + EPS) * hn_ref[h:h + 1, :]
        hh_ref[:, sl] = (y * _sigmoid(oc)).astype(hh_ref.dtype)


def _mlstm_prompt(proj, gates, hn_g, batch, seq):
    lc = REC_CHUNK
    nc = seq // lc
    hds, dh = N_REC_HEADS, LANES
    d = hds * dh
    return pl.pallas_call(
        _mlstm_prompt_kernel,
        grid=(batch, nc),
        in_specs=[
            pl.BlockSpec((lc, 4 * d), lambda b, c: (b * nc + c, 0)),
            pl.BlockSpec((lc, LANES), lambda b, c: (b * nc + c, 0)),
            pl.BlockSpec((hds, dh), lambda b, c: (0, 0)),
        ],
        out_specs=[
            pl.BlockSpec((lc, d), lambda b, c: (b * nc + c, 0)),
            pl.BlockSpec((None, hds, dh, dh), lambda b, c: (b, 0, 0, 0)),
            pl.BlockSpec((None, hds, dh), lambda b, c: (b, 0, 0)),
            pl.BlockSpec((None, hds, dh), lambda b, c: (b, 0, 0)),
        ],
        out_shape=[
            jax.ShapeDtypeStruct((batch * seq, d), BF16),
            jax.ShapeDtypeStruct((batch, hds, dh, dh), F32),
            jax.ShapeDtypeStruct((batch, hds, dh), F32),
            jax.ShapeDtypeStruct((batch, hds, dh), F32),
        ],
        compiler_params=_cp("arbitrary", "arbitrary"),
        name="mlstm_prompt",
    )(proj, gates, hn_g)


REC_STEP_BATCH = 4


def _mlstm_step_kernel(p_ref, gate_ref, hn_ref, c0_ref, n0_ref, m0_ref, hh_ref, c_ref, n_ref, m_ref):
    dh = LANES
    d = N_REC_HEADS * dh
    ri = _iota((dh, dh), 0)
    ci = _iota((dh, dh), 1)
    eye = ri == ci
    for bb in range(REC_STEP_BATCH):
        p = p_ref[bb]
        g = gate_ref[bb]
        for h in range(N_REC_HEADS):
            sl = slice(h * dh, (h + 1) * dh)
            q_row = p[:, sl]
            k_row = p[:, d + h * dh:d + (h + 1) * dh] * (dh ** -0.5)
            v_row = p[:, 2 * d + h * dh:2 * d + (h + 1) * dh]
            o_row = p[:, 3 * d + h * dh:3 * d + (h + 1) * dh]
            i_pre = g[:, h:h + 1]
            lf = _log_sigmoid(g[:, N_REC_HEADS + h:N_REC_HEADS + h + 1])
            m_old = m0_ref[bb, h:h + 1, 0:1]
            n_old = n0_ref[bb, h:h + 1, :]
            c_old = c0_ref[bb, h]
            mt = lf + jnp.maximum(m_old, i_pre - lf)
            inter = jnp.exp(m_old + lf - mt)
            w_last = jnp.exp(i_pre - mt)
            q_col = jnp.sum(jnp.where(eye, q_row, 0.0), axis=1, keepdims=True)
            k_col = jnp.sum(jnp.where(eye, k_row, 0.0), axis=1, keepdims=True)
            qc = jnp.sum(q_col * c_old, axis=0, keepdims=True)
            qk = jnp.sum(q_row * k_row, axis=-1, keepdims=True)
            qn = jnp.sum(q_row * n_old, axis=-1, keepdims=True)
            s = w_last * qk
            num = inter * qc + s * v_row
            den = inter * qn + s
            hh = num / jnp.maximum(jnp.abs(den), jnp.exp(-mt))
            c_ref[bb, h] = inter * c_old + (w_last * k_col) * v_row
            n_ref[bb, h:h + 1, :] = inter * n_old + w_last * k_row
            m_ref[bb, h:h + 1, :] = jnp.broadcast_to(mt, (1, dh))
            y = hh * lax.rsqrt(jnp.mean(hh * hh, axis=-1, keepdims=True) + EPS) * hn_ref[h:h + 1, :]
            hh_ref[bb, :, sl] = (y * _sigmoid(o_row)).astype(hh_ref.dtype)


def _mlstm_step(proj, gates, hn_g, state_c, state_n, state_m_b, j_layer):
    nb = proj.shape[0]
    hds, dh = N_REC_HEADS, LANES
    d = hds * dh
    bb = REC_STEP_BATCH
    return pl.pallas_call(
        _mlstm_step_kernel,
        grid=(nb // bb,),
        in_specs=[
            pl.BlockSpec((bb, 1, 4 * d), lambda i: (i, 0, 0)),
            pl.BlockSpec((bb, 1, LANES), lambda i: (i, 0, 0)),
            pl.BlockSpec((hds, dh), lambda i: (0, 0)),
            pl.BlockSpec((None, bb, hds, dh, dh), lambda i: (j_layer, i, 0, 0, 0)),
            pl.BlockSpec((None, bb, hds, dh), lambda i: (j_layer, i, 0, 0)),
            pl.BlockSpec((None, bb, hds, dh), lambda i: (j_layer, i, 0, 0)),
        ],
        out_specs=[
            pl.BlockSpec((bb, 1, d), lambda i: (i, 0, 0)),
            pl.BlockSpec((bb, hds, dh, dh), lambda i: (i, 0, 0, 0)),
            pl.BlockSpec((bb, hds, dh), lambda i: (i, 0, 0)),
            pl.BlockSpec((bb, hds, dh), lambda i: (i, 0, 0)),
        ],
        out_shape=[
            jax.ShapeDtypeStruct((nb, 1, d), BF16),
            jax.ShapeDtypeStruct((nb, hds, dh, dh), F32),
            jax.ShapeDtypeStruct((nb, hds, dh), F32),
            jax.ShapeDtypeStruct((nb, hds, dh), F32),
        ],
        compiler_params=_cp("arbitrary"),
        name="mlstm_step",
    )(proj.reshape(nb, 1, 4 * d), gates.reshape(nb, 1, LANES), hn_g, state_c, state_n, state_m_b)


PAGES_PER_STEP = 4


def _decode_attn_kernel(pt_ref, q_ref, kn_ref, vn_ref, slope_ref, u_ref, *refs, page, n_pages, nblk):
    pp = PAGES_PER_STEP
    k_refs, v_refs = refs[:pp], refs[pp:2 * pp]
    o_ref = refs[2 * pp]
    sb_acc, sb_carry, mb_m, mb_l, mb_acc, mb_gate = refs[2 * pp + 1:]
    g = pl.program_id(1)
    nsteps = n_pages // pp
    hm = MOBA_COLS
    scale = HEAD_DIM ** -0.5
    pos = n_pages * page
    pages_per_blk = MOBA_BLOCK // page
    blks_per_step = pp // pages_per_blk

    head = _iota((N_MOBA_HEADS, hm), 0)
    lane_head = _iota((N_MOBA_HEADS, hm), 1) // HEAD_DIM
    diag = head == lane_head
    q_mb = jnp.where(diag, q_ref[:, :hm], 0.0)
    q_sb = jnp.where(diag, q_ref[:, hm:], 0.0)
    q_mb16, q_sb16 = q_mb.astype(BF16), q_sb.astype(BF16)
    slope = slope_ref[...]
    lane_pos = _iota((1, page), 1)

    @pl.when(g == 0)
    def _():
        sb_acc[...] = jnp.zeros_like(sb_acc)
        sb_carry[...] = jnp.zeros_like(sb_carry)

    acc_sb = sb_acc[...]
    carry = sb_carry[:, 0:1]
    for half in range(blks_per_step):
        raw_parts, s_parts, v_parts = [], [], []
        for t in range(pages_per_blk):
            i = half * pages_per_blk + t
            pg = n_pages - 1 - (g * pp + i)
            z = _dot(q_sb16, k_refs[i][hm:, :].astype(BF16)) * scale
            tail = _softplus_tail(z)
            lk = -jnp.maximum(z, 0.0) - tail
            later = _dot(_hilo(lk), u_ref[...]) + carry
            w = jnp.exp(jnp.minimum(z, 0.0) - tail + later)
            acc_sb = acc_sb + _dot_nt(w.astype(BF16), v_refs[i][hm:, :].astype(BF16))
            carry = carry + jnp.sum(lk, axis=-1, keepdims=True)
            raw = _dot(q_mb16, k_refs[i][:hm, :].astype(BF16))
            dist = (pos - (pg * page + lane_pos)).astype(F32)
            raw_parts.append(raw)
            s_parts.append(raw * scale - slope * dist)
            v_parts.append(v_refs[i][:hm, :].astype(BF16))
        gate_blk = functools.reduce(lambda a, b: a + b, [jnp.sum(r, axis=-1, keepdims=True) for r in raw_parts])
        m_blk = functools.reduce(jnp.maximum, [jnp.max(s, axis=-1, keepdims=True) for s in s_parts])
        l_blk = jnp.zeros((N_MOBA_HEADS, 1), F32)
        a_blk = jnp.zeros((N_MOBA_HEADS, hm), F32)
        for s, vb in zip(s_parts, v_parts):
            p = jnp.exp(s - m_blk)
            l_blk = l_blk + jnp.sum(p, axis=-1, keepdims=True)
            a_blk = a_blk + _dot_nt(p.astype(BF16), vb)
        n_blk = nblk - 1 - (g * blks_per_step + half)
        mb_m[n_blk] = jnp.broadcast_to(m_blk, (N_MOBA_HEADS, LANES))
        mb_l[n_blk] = jnp.broadcast_to(l_blk, (N_MOBA_HEADS, LANES))
        mb_acc[n_blk] = a_blk
        mb_gate[n_blk] = jnp.broadcast_to(gate_blk * (1.0 / MOBA_BLOCK), (N_MOBA_HEADS, LANES))
    sb_acc[...] = acc_sb
    sb_carry[...] = jnp.broadcast_to(carry, (N_SB_HEADS, LANES))

    @pl.when(g == nsteps - 1)
    def _():
        gates = [mb_gate[n][:, 0:1] for n in range(nblk)]
        s_self = jnp.sum(q_mb * kn_ref[:, :hm], axis=-1, keepdims=True) * scale
        sels, m_fin = [], s_self
        for n in range(nblk):
            rank = jnp.zeros((N_MOBA_HEADS, 1), F32)
            for mth in range(nblk):
                if mth == n:
                    continue
                beats = gates[mth] > gates[n]
                if mth < n:
                    beats = beats | (gates[mth] == gates[n])
                rank = rank + beats.astype(F32)
            sel = rank < MOBA_TOPK
            sels.append(sel)
            m_fin = jnp.maximum(m_fin, jnp.where(sel, mb_m[n][:, 0:1], NEG))
        p_self = jnp.exp(s_self - m_fin)
        l_fin = p_self
        a_fin = p_self * jnp.broadcast_to(vn_ref[:, :hm], (N_MOBA_HEADS, hm))
        for n in range(nblk):
            wgt = jnp.where(sels[n], jnp.exp(mb_m[n][:, 0:1] - m_fin), 0.0)
            l_fin = l_fin + wgt * mb_l[n][:, 0:1]
            a_fin = a_fin + wgt * mb_acc[n]
        o_mb = jnp.sum(jnp.where(diag, a_fin / l_fin, 0.0), axis=0, keepdims=True)
        o_sb = jnp.sum(jnp.where(diag, acc_sb, 0.0), axis=0, keepdims=True)
        o_ref[:, :hm] = o_mb.astype(o_ref.dtype)
        o_ref[:, hm:] = o_sb.astype(o_ref.dtype)


def _decode_attn(q, k_new, v_new, cache_kt, cache_vt, page_table, j_layer):
    nb = q.shape[0]
    n_pages = page_table.shape[1]
    hd, page = cache_kt.shape[2], cache_kt.shape[3]
    hm = MOBA_COLS
    pp = PAGES_PER_STEP
    nblk = n_pages * page // MOBA_BLOCK

    def page_spec(i):
        return pl.BlockSpec((None, None, hd, page),
                            lambda b, g, pt: (j_layer, pt[b * n_pages + (n_pages - 1 - (g * pp + i))], 0, 0))

    row = pl.BlockSpec((None, 1, hd), lambda b, g, pt: (b, 0, 0))
    grid_spec = pltpu.PrefetchScalarGridSpec(
        num_scalar_prefetch=1,
        grid=(nb, n_pages // pp),
        in_specs=[row, row, row,
                  pl.BlockSpec((N_MOBA_HEADS, 1), lambda b, g, pt: (0, 0)),
                  pl.BlockSpec((2 * page, page), lambda b, g, pt: (0, 0))]
                 + [page_spec(i) for i in range(pp)] + [page_spec(i) for i in range(pp)],
        out_specs=pl.BlockSpec((None, 1, hd), lambda b, g, pt: (b, 0, 0)),
        scratch_shapes=[
            pltpu.VMEM((N_SB_HEADS, hm), F32),
            pltpu.VMEM((N_SB_HEADS, LANES), F32),
            pltpu.VMEM((nblk, N_MOBA_HEADS, LANES), F32),
            pltpu.VMEM((nblk, N_MOBA_HEADS, LANES), F32),
            pltpu.VMEM((nblk, N_MOBA_HEADS, hm), F32),
            pltpu.VMEM((nblk, N_MOBA_HEADS, LANES), F32),
        ],
    )
    out = pl.pallas_call(
        functools.partial(_decode_attn_kernel, page=page, n_pages=n_pages, nblk=nblk),
        grid_spec=grid_spec,
        out_shape=jax.ShapeDtypeStruct((nb, 1, hd), BF16),
        compiler_params=_cp("arbitrary", "arbitrary"),
        name="decode_attn",
    )(page_table.reshape(-1), q.reshape(nb, 1, hd), k_new.reshape(nb, 1, hd), v_new.reshape(nb, 1, hd),
      jnp.asarray(_alibi_slopes(), F32).reshape(N_MOBA_HEADS, 1), _later_matrix(page),
      *([cache_kt] * pp), *([cache_vt] * pp))
    return out.reshape(nb, hd)


def _decoder(x, mod_arr, rows_per_batch, tm, w, cache=None, state=None):
    m, d = x.shape
    depth = w["mlp_w1"].shape[0]
    batch = m // rows_per_batch
    kts, vts, cs, ns, ms = [], [], [], [], []
    for l in range(depth):
        mod = _Mod(mod_arr, l, d, tm, rows_per_batch)
        j = l // 2
        if l % 2 == 0:
            if cache is None:
                q, kt, vt = _qkv_proj(x, w["norm1_g"], mod, w, j, rows_per_batch, rows=False)
                oa = _moba_prompt(q, kt, vt, batch, rows_per_batch)
                ob = _sb_prompt(q, kt, vt, batch, rows_per_batch)
                o = jnp.concatenate([oa, ob], axis=1)
            else:
                q, kt, vt, k, v = _qkv_proj(x, w["norm1_g"], mod, w, j, rows_per_batch, rows=True)
                o = _decode_attn(q, k, v, cache[0], cache[1], cache[2], j)
            kts.append(kt)
            vts.append(vt)
            x = _res_matmul(o, w["att_w_out"], j, x, mod, 2, tn=1024, tk=1024)
        else:
            proj, gates = _rec_in_proj(x, w["norm1_g"], mod, w["rec_w_main"], w["rec_w_gate"], w["rec_b_gate"], j,
                                       tn=1024)
            if state is None:
                hh, c_new, n_new, m_new = _mlstm_prompt(proj, gates, w["rec_hn_g"][j], batch, rows_per_batch)
            else:
                hh, c_new, n_new, m_new = _mlstm_step(proj, gates, w["rec_hn_g"][j], state[0], state[1], state[2], j)
                hh = hh.reshape(m, d)
            cs.append(c_new)
            ns.append(n_new)
            ms.append(m_new[..., 0])
            x = _res_matmul(hh, w["rec_w_out"], j, x, mod, 2, tn=1024, tk=1024)
        hid = _nm_matmul(x, w["norm2_g"], mod, (4, 3), w["mlp_w1"], l, tn=1024, relu2=True, out_dtype=BF16)
        x = _res_matmul(hid, w["mlp_w2"], l, x, mod, 5, tn=1024, tk=1024)
    return x, jnp.stack(kts), jnp.stack(vts), jnp.stack(cs), jnp.stack(ns), jnp.stack(ms)


def _rows_from_feature_major(t):
    na, b, _, s = t.shape
    return jnp.transpose(t.reshape(na, b, N_ATT_HEADS, HEAD_DIM, s), (0, 1, 4, 2, 3))


def kernel(x_prompt, x_sample, cache_k, cache_v, page_table, state_C, state_n, state_m, c_prompt, c_sample,
           ada_w, ada_b, norm1_g, norm2_g, att_w_in, att_qn_g, att_kn_g, att_w_out,
           rec_w_in, rec_b_i, rec_b_f, rec_hn_g, rec_w_out, mlp_w1, mlp_w2):
    bp, seq, d = x_prompt.shape
    bs = x_sample.shape[0]
    depth = ada_w.shape[0]
    hds = N_REC_HEADS
    n_rec = rec_w_in.shape[0]
    hd = N_ATT_HEADS * HEAD_DIM

    mod = _adaln_mod(jnp.concatenate([c_prompt, c_sample], axis=0), ada_w, ada_b)
    mod_p = mod[:, :bp].reshape(depth, bp, 1, 6 * d)
    mod_s = mod[:, bp:]

    gate_w = jnp.pad(rec_w_in[:, :, 4 * d:], ((0, 0), (0, 0), (0, LANES - 2 * hds)))
    gate_b = jnp.pad(jnp.concatenate([rec_b_i, rec_b_f], axis=1), ((0, 0), (0, LANES - 2 * hds)))
    w_in16 = att_w_in.astype(BF16)
    w = {
        "norm1_g": norm1_g.reshape(depth, 1, d),
        "norm2_g": norm2_g.reshape(depth, 1, d),
        "att_wq": w_in16[:, :, :hd],
        "att_wk": w_in16[:, :, hd:2 * hd],
        "att_wv": w_in16[:, :, 2 * hd:],
        "att_wkt": jnp.transpose(w_in16[:, :, hd:2 * hd], (0, 2, 1)),
        "att_wvt": jnp.transpose(w_in16[:, :, 2 * hd:], (0, 2, 1)),
        "att_qn_g": jnp.tile(att_qn_g, (1, LANES // HEAD_DIM)).reshape(-1, 1, LANES),
        "att_kn_g": jnp.tile(att_kn_g, (1, LANES // HEAD_DIM)).reshape(-1, 1, LANES),
        "att_kn_gc": att_kn_g.reshape(-1, HEAD_DIM, 1),
        "att_w_out": att_w_out.astype(BF16),
        "rec_w_main": rec_w_in[:, :, :4 * d].astype(BF16),
        "rec_w_gate": gate_w.astype(BF16),
        "rec_b_gate": gate_b.reshape(n_rec, 1, LANES),
        "rec_hn_g": rec_hn_g,
        "rec_w_out": rec_w_out.astype(BF16),
        "mlp_w1": mlp_w1.astype(BF16),
        "mlp_w2": mlp_w2.astype(BF16),
    }

    yp, ktp, vtp, cp_, np_, mp = _decoder(x_prompt.reshape(bp * seq, d), mod_p, seq, 512, w)

    def feature_major_cache(c):
        nl, pool, page = c.shape[:3]
        return jnp.transpose(c, (0, 1, 3, 4, 2)).reshape(nl, pool, hd, page)

    state_m_b = jnp.broadcast_to(state_m[..., None], state_m.shape + (LANES,))
    ys, kts, vts, csm, nsm, msm = _decoder(x_sample.reshape(bs, d), mod_s, 1, bs, w,
                                           cache=(feature_major_cache(cache_k), feature_major_cache(cache_v),
                                                  page_table),
                                           state=(state_C, state_n, state_m_b))
    k_s = jnp.transpose(_rows_from_feature_major(kts), (0, 2, 1, 3, 4))
    v_s = jnp.transpose(_rows_from_feature_major(vts), (0, 2, 1, 3, 4))
    return (yp.reshape(bp, seq, d), ys.reshape(bs, 1, d),
            _rows_from_feature_major(ktp), _rows_from_feature_major(vtp), k_s, v_s,
            cp_, np_, mp, csm, nsm, msm)
```

```python
import functools

import jax
import jax.numpy as jnp
from jax import lax
from jax.experimental import pallas as pl
from jax.experimental.pallas import tpu as pltpu

F32 = jnp.float32
BF16 = jnp.bfloat16

EPS = 1e-6
HEAD_DIM = 64
N_MOBA_HEADS = 8
N_SB_HEADS = 8
N_ATT_HEADS = N_MOBA_HEADS + N_SB_HEADS
MOBA_BLOCK = 256
MOBA_TOPK = 3
N_REC_HEADS = 8
REC_CHUNK = 64
LANES = 128
NEG = -1e30
VMEM_LIMIT = 56 * 1024 * 1024
MOBA_COLS = N_MOBA_HEADS * HEAD_DIM


def _cp(*sem):
    return pltpu.CompilerParams(dimension_semantics=sem, vmem_limit_bytes=VMEM_LIMIT)


def _sigmoid(x):
    return 1.0 / (1.0 + jnp.exp(-x))


def _softplus_tail(z):
    return jnp.log(1.0 + jnp.exp(-jnp.abs(z)))


def _log_sigmoid(z):
    return jnp.minimum(z, 0.0) - _softplus_tail(z)


def _dot(a, b):
    return jnp.dot(a, b, preferred_element_type=F32)


def _dot_nt(a, b):
    return lax.dot_general(a, b, (((1,), (1,)), ((), ())), preferred_element_type=F32)


def _dot_tn(a, b):
    return lax.dot_general(a, b, (((0,), (0,)), ((), ())), preferred_element_type=F32)


def _hilo(x):
    hi = x.astype(BF16)
    lo = (x - hi.astype(F32)).astype(BF16)
    return jnp.concatenate([hi, lo], axis=-1)


def _iota(shape, dim):
    return lax.broadcasted_iota(jnp.int32, shape, dim)


def _alibi_slopes():
    return [2.0 ** (-8.0 * (h + 1) / N_MOBA_HEADS) for h in range(N_MOBA_HEADS)]


def _mod_kernel(c_ref, w_ref, b_ref, o_ref):
    c = c_ref[...]
    s = c * _sigmoid(c)
    o_ref[...] = _dot(s.astype(BF16), w_ref[...].astype(BF16)) + b_ref[...]


def _adaln_mod(c_all, ada_w, ada_b):
    depth, d, n = ada_w.shape
    m = c_all.shape[0]
    tn = 1024
    return pl.pallas_call(
        _mod_kernel,
        grid=(depth, n // tn),
        in_specs=[
            pl.BlockSpec((m, d), lambda l, j: (0, 0)),
            pl.BlockSpec((None, d, tn), lambda l, j: (l, 0, j)),
            pl.BlockSpec((None, 1, tn), lambda l, j: (l, 0, j)),
        ],
        out_specs=pl.BlockSpec((None, m, tn), lambda l, j: (l, 0, j)),
        out_shape=jax.ShapeDtypeStruct((depth, m, n), F32),
        compiler_params=_cp("arbitrary", "arbitrary"),
        name="adaln_mod",
    )(c_all, ada_w, ada_b.reshape(depth, 1, n))


class _Mod:
    def __init__(self, arr, l, d, tm, rows_per_batch):
        self.arr, self.l, self.d, self.tm = arr, l, d, tm
        self.per_row = rows_per_batch == 1
        self.blocks_per_batch = None if self.per_row else rows_per_batch // tm

    def spec(self, chunk, tn=None, with_j=False):
        d = self.d
        tn = d if tn is None else tn
        l, per = self.l, d // tn
        if self.per_row:
            if with_j:
                return pl.BlockSpec((None, self.tm, tn), lambda i, j, *_: (l, i, chunk * per + j))
            return pl.BlockSpec((None, self.tm, tn), lambda i, *_: (l, i, chunk))
        bpb = self.blocks_per_batch
        if with_j:
            return pl.BlockSpec((None, None, 1, tn), lambda i, j, *_: (l, i // bpb, 0, chunk * per + j))
        return pl.BlockSpec((None, None, 1, tn), lambda i, *_: (l, i // bpb, 0, chunk))


def _norm_mod(x_ref, g_ref, sc_ref, sh_ref):
    x = x_ref[...]
    y = x * lax.rsqrt(jnp.mean(x * x, axis=-1, keepdims=True) + EPS) * g_ref[...]
    return (y * (1.0 + sc_ref[...]) + sh_ref[...]).astype(BF16)


def _nm_mm_kernel(x_ref, g_ref, sc_ref, sh_ref, w_ref, o_ref, h_ref, *, relu2):
    @pl.when(pl.program_id(1) == 0)
    def _():
        h_ref[...] = _norm_mod(x_ref, g_ref, sc_ref, sh_ref)

    acc = _dot(h_ref[...], w_ref[...])
    if relu2:
        acc = jnp.square(jnp.maximum(acc, 0.0))
    o_ref[...] = acc.astype(o_ref.dtype)


def _nm_matmul(x, g_all, mod, chunks, w_all, l_w, *, tn, relu2, out_dtype):
    m, d = x.shape
    n = w_all.shape[-1]
    tm, l = mod.tm, mod.l
    return pl.pallas_call(
        functools.partial(_nm_mm_kernel, relu2=relu2),
        grid=(m // tm, n // tn),
        in_specs=[
            pl.BlockSpec((tm, d), lambda i, j: (i, 0)),
            pl.BlockSpec((None, 1, d), lambda i, j: (l, 0, 0)),
            mod.spec(chunks[0]),
            mod.spec(chunks[1]),
            pl.BlockSpec((None, d, tn), lambda i, j: (l_w, 0, j)),
        ],
        out_specs=pl.BlockSpec((tm, tn), lambda i, j: (i, j)),
        out_shape=jax.ShapeDtypeStruct((m, n), out_dtype),
        scratch_shapes=[pltpu.VMEM((tm, d), BF16)],
        compiler_params=_cp("arbitrary", "arbitrary"),
        name="nm_matmul",
    )(x, g_all, mod.arr, mod.arr, w_all)


def _rec_in_kernel(x_ref, g_ref, sc_ref, sh_ref, w_ref, wg_ref, bg_ref, o_ref, gate_ref, h_ref):
    @pl.when(pl.program_id(1) == 0)
    def _():
        h_ref[...] = _norm_mod(x_ref, g_ref, sc_ref, sh_ref)
        gate_ref[...] = _dot(h_ref[...], wg_ref[...]) + bg_ref[...]

    acc = _dot(h_ref[...], w_ref[...])
    for h in range(N_REC_HEADS):
        o_ref[h] = acc[:, h * LANES:(h + 1) * LANES]


def _rec_in_proj(x, g_all, mod, w_all, wg_all, bg_all, j_layer):
    m, d = x.shape
    n = w_all.shape[-1]
    tm, l = mod.tm, mod.l
    tn = N_REC_HEADS * LANES
    return pl.pallas_call(
        _rec_in_kernel,
        grid=(m // tm, n // tn),
        in_specs=[
            pl.BlockSpec((tm, d), lambda i, j: (i, 0)),
            pl.BlockSpec((None, 1, d), lambda i, j: (l, 0, 0)),
            mod.spec(1),
            mod.spec(0),
            pl.BlockSpec((None, d, tn), lambda i, j: (j_layer, 0, j)),
            pl.BlockSpec((None, d, LANES), lambda i, j: (j_layer, 0, 0)),
            pl.BlockSpec((None, 1, LANES), lambda i, j: (j_layer, 0, 0)),
        ],
        out_specs=[
            pl.BlockSpec((N_REC_HEADS, tm, LANES), lambda i, j: (j, i, 0)),
            pl.BlockSpec((tm, LANES), lambda i, j: (i, 0)),
        ],
        out_shape=[jax.ShapeDtypeStruct((n // LANES, m, LANES), F32), jax.ShapeDtypeStruct((m, LANES), F32)],
        scratch_shapes=[pltpu.VMEM((tm, d), BF16)],
        compiler_params=_cp("arbitrary", "arbitrary"),
        name="rec_in_proj",
    )(x, g_all, mod.arr, mod.arr, w_all, wg_all, bg_all)


def _head_rmsnorm_rows(a, gain, gmat):
    ms = _dot(_hilo(a * a), gmat)
    return a * lax.rsqrt(ms + EPS) * gain


def _write_normed_rows(o_ref, a, gain_ref, gm_ref):
    for c in range(MOBA_COLS // LANES):
        sl = slice(c * LANES, (c + 1) * LANES)
        o_ref[:, sl] = _head_rmsnorm_rows(a[:, sl], gain_ref[...], gm_ref[...])
    o_ref[:, MOBA_COLS:] = a[:, MOBA_COLS:]


def _qkv_kernel(x_ref, g_ref, sc_ref, sh_ref, wq_ref, wkt_ref, wvt_ref, qg_ref, kgc_ref, gm_ref, *rest, rows):
    if rows:
        wk_ref, wv_ref, kg_ref, q_ref, kt_ref, vt_ref, k_ref, v_ref = rest
    else:
        q_ref, kt_ref, vt_ref = rest
    h = _norm_mod(x_ref, g_ref, sc_ref, sh_ref)
    tm = h.shape[0]
    _write_normed_rows(q_ref, _dot(h, wq_ref[...]), qg_ref, gm_ref)
    kt = _dot_nt(wkt_ref[...], h)
    a = kt[:MOBA_COLS].reshape(N_MOBA_HEADS, HEAD_DIM, tm)
    ms = jnp.mean(a * a, axis=1, keepdims=True)
    kt_ref[:MOBA_COLS, :] = (a * lax.rsqrt(ms + EPS) * kgc_ref[...].reshape(1, HEAD_DIM, 1)).reshape(MOBA_COLS, tm)
    kt_ref[MOBA_COLS:, :] = kt[MOBA_COLS:]
    vt_ref[...] = _dot_nt(wvt_ref[...], h)
    if rows:
        _write_normed_rows(k_ref, _dot(h, wk_ref[...]), kg_ref, gm_ref)
        v_ref[...] = _dot(h, wv_ref[...])


def _group_mean_matrix():
    r = jnp.arange(2 * LANES) % LANES
    c = jnp.arange(LANES)
    return jnp.where((r[:, None] // HEAD_DIM) == (c[None, :] // HEAD_DIM), 1.0 / HEAD_DIM, 0.0).astype(BF16)


def _qkv_proj(x, g_all, mod, w, j_layer, rows_per_batch, rows):
    m, d = x.shape
    hd = N_ATT_HEADS * HEAD_DIM
    tm, l = mod.tm, mod.l
    seq = m if rows_per_batch == 1 else rows_per_batch
    batch = m // seq
    bps = seq // tm
    wspec = pl.BlockSpec((None, d, hd), lambda i: (j_layer, 0, 0))
    wtspec = pl.BlockSpec((None, hd, d), lambda i: (j_layer, 0, 0))
    gspec = pl.BlockSpec((None, 1, LANES), lambda i: (j_layer, 0, 0))
    in_specs = [
        pl.BlockSpec((tm, d), lambda i: (i, 0)),
        pl.BlockSpec((None, 1, d), lambda i: (l, 0, 0)),
        mod.spec(1),
        mod.spec(0),
        wspec, wtspec, wtspec, gspec,
        pl.BlockSpec((None, HEAD_DIM, 1), lambda i: (j_layer, 0, 0)),
        pl.BlockSpec((2 * LANES, LANES), lambda i: (0, 0)),
    ]
    args = [x, g_all, mod.arr, mod.arr, w["att_wq"], w["att_wkt"], w["att_wvt"], w["att_qn_g"], w["att_kn_gc"],
            _group_mean_matrix()]
    row_out = jax.ShapeDtypeStruct((m, hd), F32)
    t_out = jax.ShapeDtypeStruct((batch, hd, seq), F32)
    row_spec = pl.BlockSpec((tm, hd), lambda i: (i, 0))
    t_spec = pl.BlockSpec((None, hd, tm), lambda i: (i // bps, 0, i % bps))
    out_shape, out_specs = [row_out, t_out, t_out], [row_spec, t_spec, t_spec]
    if rows:
        in_specs += [wspec, wspec, gspec]
        args += [w["att_wk"], w["att_wv"], w["att_kn_g"]]
        out_shape += [row_out, row_out]
        out_specs += [row_spec, row_spec]
    return pl.pallas_call(
        functools.partial(_qkv_kernel, rows=rows),
        grid=(m // tm,),
        in_specs=in_specs,
        out_specs=out_specs,
        out_shape=out_shape,
        compiler_params=_cp("arbitrary"),
        name="qkv_proj",
    )(*args)


def _res_mm_kernel(a_ref, w_ref, r_ref, gate_ref, o_ref, acc_ref, *, nk):
    k = pl.program_id(2)

    @pl.when(k == 0)
    def _():
        acc_ref[...] = jnp.zeros_like(acc_ref)

    acc_ref[...] += _dot(a_ref[...], w_ref[...])

    @pl.when(k == nk - 1)
    def _():
        o_ref[...] = r_ref[...] + gate_ref[...] * acc_ref[...]


def _res_matmul(a, w_all, l_w, res, mod, gate_chunk, *, tn, tk):
    m, kdim = a.shape
    n = w_all.shape[-1]
    tm = mod.tm
    nk = kdim // tk
    return pl.pallas_call(
        functools.partial(_res_mm_kernel, nk=nk),
        grid=(m // tm, n // tn, nk),
        in_specs=[
            pl.BlockSpec((tm, tk), lambda i, j, k: (i, k)),
            pl.BlockSpec((None, tk, tn), lambda i, j, k: (l_w, k, j)),
            pl.BlockSpec((tm, tn), lambda i, j, k: (i, j)),
            mod.spec(gate_chunk, tn=tn, with_j=True),
        ],
        out_specs=pl.BlockSpec((tm, tn), lambda i, j, k: (i, j)),
        out_shape=jax.ShapeDtypeStruct((m, n), F32),
        scratch_shapes=[pltpu.VMEM((tm, tn), F32)],
        compiler_params=_cp("arbitrary", "arbitrary", "arbitrary"),
        name="res_matmul",
    )(a, w_all, res, mod.arr)


def _moba_prompt_kernel(q_ref, kt_ref, vt_ref, slope_ref, o_ref, *, nb):
    qi = pl.program_id(2)
    blk = MOBA_BLOCK
    scale = HEAD_DIM ** -0.5
    lane = _iota((1, LANES), 1)
    row = _iota((blk, blk), 0)
    col = _iota((blk, blk), 1)
    blk_id = _iota((blk, LANES), 1)
    blk_lane = _iota((LANES, LANES), 1)
    q = q_ref[...]
    kmean_t = jnp.zeros((LANES, LANES), F32)
    for n in range(nb):
        mean_n = jnp.mean(kt_ref[:, n * blk:(n + 1) * blk], axis=-1, keepdims=True)
        kmean_t = jnp.where(blk_lane == n, mean_n, kmean_t)
    heads = []
    for h2 in range(2):
        hmask = (lane < HEAD_DIM) if h2 == 0 else (lane >= HEAD_DIM)
        slope = slope_ref[:, h2 * HEAD_DIM:h2 * HEAD_DIM + 1]
        qf = jnp.where(hmask, q, 0.0)
        gate = jnp.dot(qf, kmean_t, precision=lax.Precision.HIGHEST, preferred_element_type=F32)
        heads.append((slope, (qf * scale).astype(BF16), jnp.where(blk_id < qi, gate, -jnp.inf)))

    def block(n, carries, own):
        start = pl.multiple_of(n * blk, blk)
        ktn = kt_ref[:, pl.ds(start, blk)].astype(BF16)
        vtn = vt_ref[:, pl.ds(start, blk)].astype(BF16)
        dist = ((qi - n) * blk + (row - col)).astype(F32)
        new = []
        for (slope, qh, gate), (m, l, acc) in zip(heads, carries):
            s = _dot(qh, ktn) - slope * dist
            if own:
                keep = col <= row
            else:
                g_n = jnp.sum(jnp.where(blk_id == n, gate, 0.0), axis=-1, keepdims=True)
                beats = (gate > g_n) | ((gate == g_n) & (blk_id < n))
                keep = jnp.sum(beats.astype(F32), axis=-1, keepdims=True) < MOBA_TOPK
            s = jnp.where(keep, s, NEG)
            m_new = jnp.maximum(m, jnp.max(s, axis=-1, keepdims=True))
            alpha = jnp.exp(m - m_new)
            p = jnp.exp(s - m_new)
            new.append((m_new, alpha * l + jnp.sum(p, axis=-1, keepdims=True),
                        alpha * acc + _dot_nt(p.astype(BF16), vtn)))
        return tuple(new)

    init = (jnp.full((blk, 1), NEG, F32), jnp.zeros((blk, 1), F32), jnp.zeros((blk, LANES), F32))
    carries = lax.fori_loop(0, qi, lambda n, c: block(n, c, False), (init, init))
    (_, l0, acc0), (_, l1, acc1) = block(qi, carries, True)
    o_ref[...] = jnp.where(lane < HEAD_DIM, acc0 / l0, acc1 / l1).astype(o_ref.dtype)


def _alibi_table():
    t = jnp.asarray(_alibi_slopes(), F32).reshape(N_MOBA_HEADS // 2, 2, 1)
    return jnp.broadcast_to(t, (N_MOBA_HEADS // 2, 2, HEAD_DIM)).reshape(N_MOBA_HEADS // 2, 1, LANES)


def _moba_prompt(q, kt, vt, batch, seq):
    nb = seq // MOBA_BLOCK
    npair = N_MOBA_HEADS // 2
    return pl.pallas_call(
        functools.partial(_moba_prompt_kernel, nb=nb),
        grid=(batch, npair, nb),
        in_specs=[
            pl.BlockSpec((MOBA_BLOCK, LANES), lambda b, hp, qi: (b * nb + qi, hp)),
            pl.BlockSpec((None, LANES, seq), lambda b, hp, qi: (b, hp, 0)),
            pl.BlockSpec((None, LANES, seq), lambda b, hp, qi: (b, hp, 0)),
            pl.BlockSpec((None, 1, LANES), lambda b, hp, qi: (hp, 0, 0)),
        ],
        out_specs=pl.BlockSpec((MOBA_BLOCK, LANES), lambda b, hp, qi: (b * nb + qi, hp)),
        out_shape=jax.ShapeDtypeStruct((batch * seq, MOBA_COLS), BF16),
        compiler_params=_cp("arbitrary", "arbitrary", "arbitrary"),
        name="moba_prompt",
    )(q, kt, vt, _alibi_table())


SB_BLOCK = 256


def _later_matrix(n):
    j = jnp.arange(2 * n) % n
    s = jnp.arange(n)
    return (j[:, None] > s[None, :]).astype(BF16)


def _sb_prompt_kernel(q_ref, kt_ref, vt_ref, u_ref, o_ref):
    qi = pl.program_id(2)
    blk = SB_BLOCK
    scale = HEAD_DIM ** -0.5
    lane = _iota((1, LANES), 1)
    row = _iota((blk, blk), 0)
    col = _iota((blk, blk), 1)
    q = q_ref[...] * scale
    qhs = [jnp.where(lane < HEAD_DIM, q, 0.0).astype(BF16), jnp.where(lane >= HEAD_DIM, q, 0.0).astype(BF16)]

    def block(n, state, mask):
        start = pl.multiple_of(n * blk, blk)
        ktn = kt_ref[:, pl.ds(start, blk)].astype(BF16)
        vtn = vt_ref[:, pl.ds(start, blk)].astype(BF16)
        new = []
        for qh, (carry, acc) in zip(qhs, state):
            z = _dot(qh, ktn)
            tail = _softplus_tail(z)
            lk = -jnp.maximum(z, 0.0) - tail
            ls = jnp.minimum(z, 0.0) - tail
            if mask is not None:
                lk = jnp.where(mask, lk, 0.0)
            later = _dot(_hilo(lk), u_ref[...]) + carry
            w = jnp.exp(ls + later)
            if mask is not None:
                w = jnp.where(mask, w, 0.0)
            new.append((carry + jnp.sum(lk, axis=-1, keepdims=True), acc + _dot_nt(w.astype(BF16), vtn)))
        return tuple(new)

    init = (jnp.zeros((blk, 1), F32), jnp.zeros((blk, LANES), F32))
    state = block(qi, (init, init), col < row)
    (_, acc0), (_, acc1) = lax.fori_loop(0, qi, lambda r, st: block(qi - 1 - r, st, None), state)
    o_ref[...] = jnp.where(lane < HEAD_DIM, acc0, acc1).astype(o_ref.dtype)


def _sb_prompt(q, kt, vt, batch, seq):
    nq = seq // SB_BLOCK
    npair = N_SB_HEADS // 2
    off = N_MOBA_HEADS // 2
    return pl.pallas_call(
        _sb_prompt_kernel,
        grid=(batch, npair, nq),
        in_specs=[
            pl.BlockSpec((SB_BLOCK, LANES), lambda b, hp, qi: (b * nq + qi, off + hp)),
            pl.BlockSpec((None, LANES, seq), lambda b, hp, qi: (b, off + hp, 0)),
            pl.BlockSpec((None, LANES, seq), lambda b, hp, qi: (b, off + hp, 0)),
            pl.BlockSpec((2 * SB_BLOCK, SB_BLOCK), lambda b, hp, qi: (0, 0)),
        ],
        out_specs=pl.BlockSpec((SB_BLOCK, LANES), lambda b, hp, qi: (b * nq + qi, hp)),
        out_shape=jax.ShapeDtypeStruct((batch * seq, N_SB_HEADS * HEAD_DIM), BF16),
        compiler_params=_cp("arbitrary", "arbitrary", "arbitrary"),
        name="sb_prompt",
    )(q, kt, vt, _later_matrix(SB_BLOCK))


def _head_columns(g, first):
    return jnp.stack([g[:, first + h:first + h + 1] for h in range(N_REC_HEADS)])


def _mlstm_prompt_kernel(q_ref, k_ref, v_ref, o_ref, gate_ref, hn_ref, hh_ref, c_ref, n_ref, m_ref):
    lc = REC_CHUNK
    dh = LANES
    hds = N_REC_HEADS

    @pl.when(pl.program_id(1) == 0)
    def _():
        c_ref[...] = jnp.zeros_like(c_ref)
        n_ref[...] = jnp.zeros_like(n_ref)
        m_ref[...] = jnp.zeros_like(m_ref)

    ri = _iota((1, lc, lc), 1)
    ci = _iota((1, lc, lc), 2)
    eye = ri == ci
    tril = ci <= ri
    q3 = q_ref[...]
    k3 = k_ref[...] * (dh ** -0.5)
    v3 = v_ref[...]
    g = gate_ref[...]
    i_col = _head_columns(g, 0)
    lf_col = _log_sigmoid(_head_columns(g, hds))
    c_old = c_ref[...]
    n_old = n_ref[...]
    m_old = m_ref[...]
    qb, kb, vb = q3.astype(BF16), k3.astype(BF16), v3.astype(BF16)
    cb = c_old.astype(BF16)

    lf_row = jnp.sum(jnp.where(eye, lf_col, 0.0), axis=1, keepdims=True)
    i_row = jnp.sum(jnp.where(eye, i_col, 0.0), axis=1, keepdims=True)
    b_col = jnp.sum(jnp.where(tril, lf_row, 0.0), axis=2, keepdims=True)
    b_row = jnp.sum(jnp.where(ri <= ci, lf_col, 0.0), axis=1, keepdims=True)
    a_row = i_row - b_row
    a_col = i_col - b_col
    cm_col = jnp.max(jnp.where(tril, a_row, -jnp.inf), axis=2, keepdims=True)
    mt_col = b_col + jnp.maximum(m_old, cm_col)
    inter = jnp.exp(m_old + b_col - mt_col)
    dmat = jnp.where(tril, jnp.exp(jnp.where(tril, a_row + (b_col - mt_col), 0.0)), 0.0)
    s = dmat * jnp.stack([_dot_nt(qb[h], kb[h]) for h in range(hds)])
    sb = s.astype(BF16)
    num = inter * jnp.stack([_dot(qb[h], cb[h]) for h in range(hds)]) \
        + jnp.stack([_dot(sb[h], vb[h]) for h in range(hds)])
    den = inter * jnp.sum(q3 * n_old, axis=2, keepdims=True) + jnp.sum(s, axis=2, keepdims=True)
    hh = num / jnp.maximum(jnp.abs(den), jnp.exp(-mt_col))
    b_last = b_col[:, lc - 1:lc, :]
    m_new = mt_col[:, lc - 1:lc, :]
    w_last = jnp.exp(a_col + (b_last - m_new))
    decay = jnp.exp(m_old + b_last - m_new)
    kw = k3 * w_last
    kwb = kw.astype(BF16)
    c_ref[...] = decay * c_old + jnp.stack([_dot_tn(kwb[h], vb[h]) for h in range(hds)])
    n_ref[...] = decay * n_old + jnp.sum(kw, axis=1, keepdims=True)
    m_ref[...] = m_new

    y = hh * lax.rsqrt(jnp.mean(hh * hh, axis=2, keepdims=True) + EPS) * hn_ref[...]
    out = (y * _sigmoid(o_ref[...])).astype(hh_ref.dtype)
    for h in range(hds):
        hh_ref[:, h * dh:(h + 1) * dh] = out[h]


def _mlstm_prompt(proj, gates, hn_g, batch, seq):
    lc = REC_CHUNK
    nc = seq // lc
    hds, dh = N_REC_HEADS, LANES
    d = hds * dh

    def part(kind):
        return pl.BlockSpec((hds, lc, dh), lambda b, c: (kind, b * nc + c, 0))

    return pl.pallas_call(
        _mlstm_prompt_kernel,
        grid=(batch, nc),
        in_specs=[
            part(0), part(1), part(2), part(3),
            pl.BlockSpec((lc, LANES), lambda b, c: (b * nc + c, 0)),
            pl.BlockSpec((hds, 1, dh), lambda b, c: (0, 0, 0)),
        ],
        out_specs=[
            pl.BlockSpec((lc, d), lambda b, c: (b * nc + c, 0)),
            pl.BlockSpec((None, hds, dh, dh), lambda b, c: (b, 0, 0, 0)),
            pl.BlockSpec((None, hds, 1, dh), lambda b, c: (b, 0, 0, 0)),
            pl.BlockSpec((None, hds, 1, 1), lambda b, c: (b, 0, 0, 0)),
        ],
        out_shape=[
            jax.ShapeDtypeStruct((batch * seq, d), BF16),
            jax.ShapeDtypeStruct((batch, hds, dh, dh), F32),
            jax.ShapeDtypeStruct((batch, hds, 1, dh), F32),
            jax.ShapeDtypeStruct((batch, hds, 1, 1), F32),
        ],
        compiler_params=_cp("arbitrary", "arbitrary"),
        name="mlstm_prompt",
    )(proj, proj, proj, proj, gates, hn_g.reshape(hds, 1, dh))


REC_STEP_BATCH = 8


def _mlstm_step_kernel(p_ref, gate_ref, hn_ref, c0_ref, n0_ref, m0_ref, hh_ref, c_ref, n_ref, m_ref):
    dh = LANES
    hds = N_REC_HEADS
    eye = _iota((1, dh, dh), 1) == _iota((1, dh, dh), 2)
    hn = hn_ref[...]
    for bb in range(REC_STEP_BATCH):
        row = slice(bb, bb + 1)
        q3 = p_ref[0:hds, row, :]
        k3 = p_ref[hds:2 * hds, row, :] * (dh ** -0.5)
        v3 = p_ref[2 * hds:3 * hds, row, :]
        o3 = p_ref[3 * hds:4 * hds, row, :]
        g = gate_ref[row, :]
        i_pre = _head_columns(g, 0)
        lf = _log_sigmoid(_head_columns(g, hds))
        m_old = m0_ref[bb]
        n_old = n0_ref[bb]
        c_old = c0_ref[bb]
        mt = lf + jnp.maximum(m_old, i_pre - lf)
        inter = jnp.exp(m_old + lf - mt)
        w_last = jnp.exp(i_pre - mt)
        q_col = jnp.sum(jnp.where(eye, q3, 0.0), axis=2, keepdims=True)
        k_col = jnp.sum(jnp.where(eye, k3, 0.0), axis=2, keepdims=True)
        qc = jnp.sum(q_col * c_old, axis=1, keepdims=True)
        qk = jnp.sum(q3 * k3, axis=2, keepdims=True)
        qn = jnp.sum(q3 * n_old, axis=2, keepdims=True)
        s = w_last * qk
        num = inter * qc + s * v3
        den = inter * qn + s
        hh = num / jnp.maximum(jnp.abs(den), jnp.exp(-mt))
        c_ref[bb] = inter * c_old + (w_last * k_col) * v3
        n_ref[bb] = inter * n_old + w_last * k3
        m_ref[bb] = mt
        y = hh * lax.rsqrt(jnp.mean(hh * hh, axis=2, keepdims=True) + EPS) * hn
        out = (y * _sigmoid(o3)).astype(hh_ref.dtype)
        for h in range(hds):
            hh_ref[row, h * dh:(h + 1) * dh] = out[h]


def _mlstm_step(proj, gates, hn_g, state_c, state_n, state_m, j_layer):
    nb = proj.shape[1]
    hds, dh = N_REC_HEADS, LANES
    d = hds * dh
    bb = REC_STEP_BATCH
    return pl.pallas_call(
        _mlstm_step_kernel,
        grid=(nb // bb,),
        in_specs=[
            pl.BlockSpec((4 * hds, bb, dh), lambda i: (0, i, 0)),
            pl.BlockSpec((bb, LANES), lambda i: (i, 0)),
            pl.BlockSpec((hds, 1, dh), lambda i: (0, 0, 0)),
            pl.BlockSpec((None, bb, hds, dh, dh), lambda i: (j_layer, i, 0, 0, 0)),
            pl.BlockSpec((None, bb, hds, 1, dh), lambda i: (j_layer, i, 0, 0, 0)),
            pl.BlockSpec((None, bb, hds, 1, 1), lambda i: (j_layer, i, 0, 0, 0)),
        ],
        out_specs=[
            pl.BlockSpec((bb, d), lambda i: (i, 0)),
            pl.BlockSpec((bb, hds, dh, dh), lambda i: (i, 0, 0, 0)),
            pl.BlockSpec((bb, hds, 1, dh), lambda i: (i, 0, 0, 0)),
            pl.BlockSpec((bb, hds, 1, 1), lambda i: (i, 0, 0, 0)),
        ],
        out_shape=[
            jax.ShapeDtypeStruct((nb, d), BF16),
            jax.ShapeDtypeStruct((nb, hds, dh, dh), F32),
            jax.ShapeDtypeStruct((nb, hds, 1, dh), F32),
            jax.ShapeDtypeStruct((nb, hds, 1, 1), F32),
        ],
        compiler_params=_cp("arbitrary"),
        name="mlstm_step",
    )(proj, gates, hn_g.reshape(hds, 1, dh), state_c, state_n, state_m)


PAGES_PER_STEP = 4


def _spread_heads(x):
    h, l = x.shape
    return jnp.broadcast_to(x[:, None, :], (h, HEAD_DIM, l)).reshape(h * HEAD_DIM, l)


def _row_to_column(r):
    eye = _iota((LANES, LANES), 0) == _iota((LANES, LANES), 1)
    return jnp.sum(jnp.where(eye, r, 0.0), axis=1, keepdims=True)


def _column_to_row(c):
    eye = _iota((LANES, LANES), 0) == _iota((LANES, LANES), 1)
    return jnp.sum(jnp.where(eye, c, 0.0), axis=0, keepdims=True)


def _decode_attn_kernel(pt_ref, q_ref, kn_ref, vn_ref, slope_ref, u_ref, *refs, page, n_pages, nblk):
    pp = PAGES_PER_STEP
    k_refs, v_refs = refs[:pp], refs[pp:2 * pp]
    o_ref = refs[2 * pp]
    q_bc, sb_acc, sb_carry, mb_m, mb_l, mb_acc, mb_gate = refs[2 * pp + 1:]
    g = pl.program_id(1)
    nsteps = n_pages // pp
    hm = MOBA_COLS
    hd = N_ATT_HEADS * HEAD_DIM
    scale = HEAD_DIM ** -0.5
    pos = n_pages * page
    pages_per_blk = MOBA_BLOCK // page
    blks_per_step = pp // pages_per_blk
    slope = slope_ref[...]
    lane_pos = _iota((1, page), 1)

    @pl.when(g == 0)
    def _():
        for c in range(hd // LANES):
            sl = slice(c * LANES, (c + 1) * LANES)
            q_bc[sl, :] = jnp.broadcast_to(_row_to_column(q_ref[:, sl]), (LANES, page))
        sb_acc[...] = jnp.zeros_like(sb_acc)
        sb_carry[...] = jnp.zeros_like(sb_carry)

    q_all = q_bc[...]
    raws, zs = [], []
    for i in range(pp):
        sc = jnp.sum((k_refs[i][...] * q_all).reshape(N_ATT_HEADS, HEAD_DIM, page), axis=1)
        raws.append(sc[:N_MOBA_HEADS])
        zs.append(sc[N_MOBA_HEADS:] * scale)

    tails = [_softplus_tail(z) for z in zs]
    lks = [-jnp.maximum(z, 0.0) - t for z, t in zip(zs, tails)]
    later_all = _dot(_hilo(jnp.concatenate(lks, axis=0)), u_ref[...])
    carry = sb_carry[:, 0:1]
    acc_sb = sb_acc[...]
    for i in range(pp):
        later = later_all[i * N_SB_HEADS:(i + 1) * N_SB_HEADS] + carry
        w = jnp.exp(jnp.minimum(zs[i], 0.0) - tails[i] + later)
        acc_sb = acc_sb + v_refs[i][hm:, :] * _spread_heads(w)
        carry = carry + jnp.sum(lks[i], axis=-1, keepdims=True)
    sb_acc[...] = acc_sb
    sb_carry[...] = jnp.broadcast_to(carry, (N_SB_HEADS, LANES))

    for half in range(blks_per_step):
        idx = [half * pages_per_blk + t for t in range(pages_per_blk)]
        s_parts = []
        for i in idx:
            pg = n_pages - 1 - (g * pp + i)
            dist = (pos - (pg * page + lane_pos)).astype(F32)
            s_parts.append(raws[i] * scale - slope * dist)
        gate_blk = functools.reduce(lambda a, b: a + b, [jnp.sum(raws[i], axis=-1, keepdims=True) for i in idx])
        m_blk = functools.reduce(jnp.maximum, [jnp.max(s, axis=-1, keepdims=True) for s in s_parts])
        l_blk = jnp.zeros((N_MOBA_HEADS, 1), F32)
        a_blk = jnp.zeros((hm, page), F32)
        for i, s in zip(idx, s_parts):
            p = jnp.exp(s - m_blk)
            l_blk = l_blk + jnp.sum(p, axis=-1, keepdims=True)
            a_blk = a_blk + v_refs[i][:hm, :] * _spread_heads(p)
        n_blk = nblk - 1 - (g * blks_per_step + half)
        mb_m[n_blk] = jnp.broadcast_to(m_blk, (N_MOBA_HEADS, LANES))
        mb_l[n_blk] = jnp.broadcast_to(l_blk, (N_MOBA_HEADS, LANES))
        mb_acc[n_blk] = a_blk
        mb_gate[n_blk] = jnp.broadcast_to(gate_blk * (1.0 / MOBA_BLOCK), (N_MOBA_HEADS, LANES))

    @pl.when(g == nsteps - 1)
    def _():
        gates = [mb_gate[n][:, 0:1] for n in range(nblk)]
        diag = _iota((N_MOBA_HEADS, hm), 0) == _iota((N_MOBA_HEADS, hm), 1) // HEAD_DIM
        q_mb = jnp.where(diag, q_ref[:, :hm], 0.0)
        s_self = jnp.sum(q_mb * kn_ref[:, :hm], axis=-1, keepdims=True) * scale
        sels, m_fin = [], s_self
        for n in range(nblk):
            rank = jnp.zeros((N_MOBA_HEADS, 1), F32)
            for mth in range(nblk):
                if mth == n:
                    continue
                beats = gates[mth] > gates[n]
                if mth < n:
                    beats = beats | (gates[mth] == gates[n])
                rank = rank + beats.astype(F32)
            sel = rank < MOBA_TOPK
            sels.append(sel)
            m_fin = jnp.maximum(m_fin, jnp.where(sel, mb_m[n][:, 0:1], NEG))
        p_self = jnp.exp(s_self - m_fin)
        l_fin = p_self
        a_fin = jnp.zeros((hm, page), F32)
        for n in range(nblk):
            wgt = jnp.where(sels[n], jnp.exp(mb_m[n][:, 0:1] - m_fin), 0.0)
            l_fin = l_fin + wgt * mb_l[n][:, 0:1]
            a_fin = a_fin + _spread_heads(wgt) * mb_acc[n]
        col_mb = jnp.sum(a_fin, axis=-1, keepdims=True)
        col_sb = jnp.sum(acc_sb, axis=-1, keepdims=True)
        w_self = _spread_heads(p_self)
        inv_l = _spread_heads(1.0 / l_fin)
        for c in range(hm // LANES):
            sl = slice(c * LANES, (c + 1) * LANES)
            v_col = _row_to_column(vn_ref[:, sl])
            o_ref[:, sl] = _column_to_row((col_mb[sl] + w_self[sl] * v_col) * inv_l[sl]).astype(o_ref.dtype)
            o_ref[:, hm + c * LANES:hm + (c + 1) * LANES] = _column_to_row(col_sb[sl]).astype(o_ref.dtype)


def _decode_attn(q, k_new, v_new, cache_kt, cache_vt, page_table, j_layer):
    nb = q.shape[0]
    n_pages = page_table.shape[1]
    hd, page = cache_kt.shape[2], cache_kt.shape[3]
    hm = MOBA_COLS
    pp = PAGES_PER_STEP
    nblk = n_pages * page // MOBA_BLOCK

    def page_spec(i):
        return pl.BlockSpec((None, None, hd, page),
                            lambda b, g, pt: (j_layer, pt[b * n_pages + (n_pages - 1 - (g * pp + i))], 0, 0))

    row = pl.BlockSpec((None, 1, hd), lambda b, g, pt: (b, 0, 0))
    grid_spec = pltpu.PrefetchScalarGridSpec(
        num_scalar_prefetch=1,
        grid=(nb, n_pages // pp),
        in_specs=[row, row, row,
                  pl.BlockSpec((N_MOBA_HEADS, 1), lambda b, g, pt: (0, 0)),
                  pl.BlockSpec((2 * page, page), lambda b, g, pt: (0, 0))]
                 + [page_spec(i) for i in range(pp)] + [page_spec(i) for i in range(pp)],
        out_specs=pl.BlockSpec((None, 1, hd), lambda b, g, pt: (b, 0, 0)),
        scratch_shapes=[
            pltpu.VMEM((hd, page), F32),
            pltpu.VMEM((hd - hm, page), F32),
            pltpu.VMEM((N_SB_HEADS, LANES), F32),
            pltpu.VMEM((nblk, N_MOBA_HEADS, LANES), F32),
            pltpu.VMEM((nblk, N_MOBA_HEADS, LANES), F32),
            pltpu.VMEM((nblk, hm, page), F32),
            pltpu.VMEM((nblk, N_MOBA_HEADS, LANES), F32),
        ],
    )
    out = pl.pallas_call(
        functools.partial(_decode_attn_kernel, page=page, n_pages=n_pages, nblk=nblk),
        grid_spec=grid_spec,
        out_shape=jax.ShapeDtypeStruct((nb, 1, hd), BF16),
        compiler_params=_cp("arbitrary", "arbitrary"),
        name="decode_attn",
    )(page_table.reshape(-1), q.reshape(nb, 1, hd), k_new.reshape(nb, 1, hd), v_new.reshape(nb, 1, hd),
      jnp.asarray(_alibi_slopes(), F32).reshape(N_MOBA_HEADS, 1), _later_matrix(page),
      *([cache_kt] * pp), *([cache_vt] * pp))
    return out.reshape(nb, hd)


def _row_tiles(m, rows_per_batch):
    if rows_per_batch == 1:
        return m, m
    return min(1024, rows_per_batch), min(512, rows_per_batch)


def _decoder(x, mod_arr, rows_per_batch, w, cache=None, state=None):
    m, d = x.shape
    depth = w["mlp_w1"].shape[0]
    batch = m // rows_per_batch
    tm, tm_qkv = _row_tiles(m, rows_per_batch)
    tk_ff = min(2048, w["mlp_w2"].shape[1])
    kts, vts, cs, ns, ms = [], [], [], [], []
    for l in range(depth):
        mod = _Mod(mod_arr, l, d, tm, rows_per_batch)
        mod_qkv = _Mod(mod_arr, l, d, tm_qkv, rows_per_batch)
        j = l // 2
        if l % 2 == 0:
            if cache is None:
                q, kt, vt = _qkv_proj(x, w["norm1_g"], mod_qkv, w, j, rows_per_batch, rows=False)
                oa = _moba_prompt(q, kt, vt, batch, rows_per_batch)
                ob = _sb_prompt(q, kt, vt, batch, rows_per_batch)
                o = jnp.concatenate([oa, ob], axis=1)
            else:
                q, kt, vt, k, v = _qkv_proj(x, w["norm1_g"], mod_qkv, w, j, rows_per_batch, rows=True)
                o = _decode_attn(q, k, v, cache[0], cache[1], cache[2], j)
            kts.append(kt)
            vts.append(vt)
            x = _res_matmul(o, w["att_w_out"], j, x, mod, 2, tn=1024, tk=1024)
        else:
            proj, gates = _rec_in_proj(x, w["norm1_g"], mod, w["rec_w_main"], w["rec_w_gate"], w["rec_b_gate"], j)
            if state is None:
                hh, c_new, n_new, m_new = _mlstm_prompt(proj, gates, w["rec_hn_g"][j], batch, rows_per_batch)
            else:
                hh, c_new, n_new, m_new = _mlstm_step(proj, gates, w["rec_hn_g"][j], state[0], state[1], state[2], j)
            cs.append(c_new)
            ns.append(n_new.reshape(batch, N_REC_HEADS, LANES))
            ms.append(m_new.reshape(batch, N_REC_HEADS))
            x = _res_matmul(hh, w["rec_w_out"], j, x, mod, 2, tn=1024, tk=1024)
        hid = _nm_matmul(x, w["norm2_g"], mod, (4, 3), w["mlp_w1"], l, tn=1024, relu2=True, out_dtype=BF16)
        x = _res_matmul(hid, w["mlp_w2"], l, x, mod, 5, tn=1024, tk=tk_ff)
    return x, jnp.stack(kts), jnp.stack(vts), jnp.stack(cs), jnp.stack(ns), jnp.stack(ms)


def _rows_from_feature_major(t):
    na, b, _, s = t.shape
    return jnp.transpose(t.reshape(na, b, N_ATT_HEADS, HEAD_DIM, s), (0, 1, 4, 2, 3))


def kernel(x_prompt, x_sample, cache_k, cache_v, page_table, state_C, state_n, state_m, c_prompt, c_sample,
           ada_w, ada_b, norm1_g, norm2_g, att_w_in, att_qn_g, att_kn_g, att_w_out,
           rec_w_in, rec_b_i, rec_b_f, rec_hn_g, rec_w_out, mlp_w1, mlp_w2):
    bp, seq, d = x_prompt.shape
    bs = x_sample.shape[0]
    depth = ada_w.shape[0]
    hds = N_REC_HEADS
    n_rec = rec_w_in.shape[0]
    hd = N_ATT_HEADS * HEAD_DIM

    mod = _adaln_mod(jnp.concatenate([c_prompt, c_sample], axis=0), ada_w, ada_b)
    mod_p = mod[:, :bp].reshape(depth, bp, 1, 6 * d)
    mod_s = mod[:, bp:]

    gate_w = jnp.pad(rec_w_in[:, :, 4 * d:], ((0, 0), (0, 0), (0, LANES - 2 * hds)))
    gate_b = jnp.pad(jnp.concatenate([rec_b_i, rec_b_f], axis=1), ((0, 0), (0, LANES - 2 * hds)))
    w_in16 = att_w_in.astype(BF16)
    w = {
        "norm1_g": norm1_g.reshape(depth, 1, d),
        "norm2_g": norm2_g.reshape(depth, 1, d),
        "att_wq": w_in16[:, :, :hd],
        "att_wk": w_in16[:, :, hd:2 * hd],
        "att_wv": w_in16[:, :, 2 * hd:],
        "att_wkt": jnp.transpose(w_in16[:, :, hd:2 * hd], (0, 2, 1)),
        "att_wvt": jnp.transpose(w_in16[:, :, 2 * hd:], (0, 2, 1)),
        "att_qn_g": jnp.tile(att_qn_g, (1, LANES // HEAD_DIM)).reshape(-1, 1, LANES),
        "att_kn_g": jnp.tile(att_kn_g, (1, LANES // HEAD_DIM)).reshape(-1, 1, LANES),
        "att_kn_gc": att_kn_g.reshape(-1, HEAD_DIM, 1),
        "att_w_out": att_w_out.astype(BF16),
        "rec_w_main": rec_w_in[:, :, :4 * d].astype(BF16),
        "rec_w_gate": gate_w.astype(BF16),
        "rec_b_gate": gate_b.reshape(n_rec, 1, LANES),
        "rec_hn_g": rec_hn_g,
        "rec_w_out": rec_w_out.astype(BF16),
        "mlp_w1": mlp_w1.astype(BF16),
        "mlp_w2": mlp_w2.astype(BF16),
    }

    yp, ktp, vtp, cp_, np_, mp = _decoder(x_prompt.reshape(bp * seq, d), mod_p, seq, w)

    def feature_major_cache(c):
        nl, pool, page = c.shape[:3]
        return jnp.transpose(c, (0, 1, 3, 4, 2)).reshape(nl, pool, hd, page)

    ys, kts, vts, csm, nsm, msm = _decoder(x_sample.reshape(bs, d), mod_s, 1, w,
                                           cache=(feature_major_cache(cache_k), feature_major_cache(cache_v),
                                                  page_table),
                                           state=(state_C, state_n.reshape(n_rec, bs, hds, 1, LANES),
                                                  state_m.reshape(n_rec, bs, hds, 1, 1)))
    k_s = jnp.transpose(_rows_from_feature_major(kts), (0, 2, 1, 3, 4))
    v_s = jnp.transpose(_rows_from_feature_major(vts), (0, 2, 1, 3, 4))
    return (yp.reshape(bp, seq, d), ys.reshape(bs, 1, d),
            _rows_from_feature_major(ktp), _rows_from_feature_major(vtp), k_s, v_s,
            cp_, np_, mp, csm, nsm, msm)
```

```python
import functools

import jax
import jax.numpy as jnp
from jax import lax
from jax.experimental import pallas as pl
from jax.experimental.pallas import tpu as pltpu

F32 = jnp.float32
BF16 = jnp.bfloat16

EPS = 1e-6
HEAD_DIM = 64
N_MOBA_HEADS = 8
N_SB_HEADS = 8
N_ATT_HEADS = N_MOBA_HEADS + N_SB_HEADS
MOBA_BLOCK = 256
MOBA_TOPK = 3
N_REC_HEADS = 8
REC_CHUNK = 64
LANES = 128
NEG = -1e30
VMEM_LIMIT = 56 * 1024 * 1024
MOBA_COLS = N_MOBA_HEADS * HEAD_DIM


def _cp(*sem):
    return pltpu.CompilerParams(dimension_semantics=sem, vmem_limit_bytes=VMEM_LIMIT)


def _sigmoid(x):
    return 1.0 / (1.0 + jnp.exp(-x))


def _softplus_tail(z):
    return jnp.log(1.0 + jnp.exp(-jnp.abs(z)))


def _log_sigmoid(z):
    return jnp.minimum(z, 0.0) - _softplus_tail(z)


def _dot(a, b):
    return jnp.dot(a, b, preferred_element_type=F32)


def _dot_nt(a, b):
    return lax.dot_general(a, b, (((1,), (1,)), ((), ())), preferred_element_type=F32)


def _dot_tn(a, b):
    return lax.dot_general(a, b, (((0,), (0,)), ((), ())), preferred_element_type=F32)


def _hilo(x):
    hi = x.astype(BF16)
    lo = (x - hi.astype(F32)).astype(BF16)
    return jnp.concatenate([hi, lo], axis=-1)


def _iota(shape, dim):
    return lax.broadcasted_iota(jnp.int32, shape, dim)


def _alibi_slopes():
    return [2.0 ** (-8.0 * (h + 1) / N_MOBA_HEADS) for h in range(N_MOBA_HEADS)]


def _mod_kernel(c_ref, w_ref, b_ref, o_ref):
    c = c_ref[...]
    s = c * _sigmoid(c)
    o_ref[...] = _dot(s.astype(BF16), w_ref[...].astype(BF16)) + b_ref[...]


def _adaln_mod(c_all, ada_w, ada_b):
    depth, d, n = ada_w.shape
    m = c_all.shape[0]
    tn = 1024
    return pl.pallas_call(
        _mod_kernel,
        grid=(depth, n // tn),
        in_specs=[
            pl.BlockSpec((m, d), lambda l, j: (0, 0)),
            pl.BlockSpec((None, d, tn), lambda l, j: (l, 0, j)),
            pl.BlockSpec((None, 1, tn), lambda l, j: (l, 0, j)),
        ],
        out_specs=pl.BlockSpec((None, m, tn), lambda l, j: (l, 0, j)),
        out_shape=jax.ShapeDtypeStruct((depth, m, n), F32),
        compiler_params=_cp("arbitrary", "arbitrary"),
        name="adaln_mod",
    )(c_all, ada_w, ada_b.reshape(depth, 1, n))


class _Mod:
    def __init__(self, arr, l, d, tm, rows_per_batch):
        self.arr, self.l, self.d, self.tm = arr, l, d, tm
        self.per_row = rows_per_batch == 1
        self.blocks_per_batch = None if self.per_row else rows_per_batch // tm

    def spec(self, chunk):
        d, l = self.d, self.l
        if self.per_row:
            return pl.BlockSpec((None, self.tm, d), lambda i: (l, i, chunk))
        bpb = self.blocks_per_batch
        return pl.BlockSpec((None, None, 1, d), lambda i: (l, i // bpb, 0, chunk))


def _norm_mod(x_ref, g_ref, sc_ref, sh_ref):
    x = x_ref[...]
    y = x * lax.rsqrt(jnp.mean(x * x, axis=-1, keepdims=True) + EPS) * g_ref[...]
    return (y * (1.0 + sc_ref[...]) + sh_ref[...]).astype(BF16)


COL_CHUNK = 1024


def _nm_mm_kernel(x_ref, g_ref, sc_ref, sh_ref, w_ref, o_ref, *, relu2):
    h = _norm_mod(x_ref, g_ref, sc_ref, sh_ref)
    for c in range(w_ref.shape[1] // COL_CHUNK):
        sl = slice(c * COL_CHUNK, (c + 1) * COL_CHUNK)
        acc = _dot(h, w_ref[:, sl])
        if relu2:
            acc = jnp.square(jnp.maximum(acc, 0.0))
        o_ref[:, sl] = acc.astype(o_ref.dtype)


def _nm_matmul(x, g_all, mod, chunks, w_all, l_w, *, relu2, out_dtype):
    m, d = x.shape
    n = w_all.shape[-1]
    tm, l = mod.tm, mod.l
    return pl.pallas_call(
        functools.partial(_nm_mm_kernel, relu2=relu2),
        grid=(m // tm,),
        in_specs=[
            pl.BlockSpec((tm, d), lambda i: (i, 0)),
            pl.BlockSpec((None, 1, d), lambda i: (l, 0, 0)),
            mod.spec(chunks[0]),
            mod.spec(chunks[1]),
            pl.BlockSpec((None, d, n), lambda i: (l_w, 0, 0)),
        ],
        out_specs=pl.BlockSpec((tm, n), lambda i: (i, 0)),
        out_shape=jax.ShapeDtypeStruct((m, n), out_dtype),
        compiler_params=_cp("arbitrary"),
        name="nm_matmul",
    )(x, g_all, mod.arr, mod.arr, w_all)


def _rec_in_kernel(x_ref, g_ref, sc_ref, sh_ref, w_ref, wg_ref, bg_ref, o_ref, gate_ref):
    h = _norm_mod(x_ref, g_ref, sc_ref, sh_ref)
    gate_ref[...] = _dot(h, wg_ref[...]) + bg_ref[...]
    dm = N_REC_HEADS * LANES
    for kind in range(w_ref.shape[1] // dm):
        acc = _dot(h, w_ref[:, kind * dm:(kind + 1) * dm])
        for hh in range(N_REC_HEADS):
            o_ref[kind * N_REC_HEADS + hh] = acc[:, hh * LANES:(hh + 1) * LANES]


def _rec_in_proj(x, g_all, mod, w_all, wg_all, bg_all, j_layer):
    m, d = x.shape
    n = w_all.shape[-1]
    tm, l = mod.tm, mod.l
    return pl.pallas_call(
        _rec_in_kernel,
        grid=(m // tm,),
        in_specs=[
            pl.BlockSpec((tm, d), lambda i: (i, 0)),
            pl.BlockSpec((None, 1, d), lambda i: (l, 0, 0)),
            mod.spec(1),
            mod.spec(0),
            pl.BlockSpec((None, d, n), lambda i: (j_layer, 0, 0)),
            pl.BlockSpec((None, d, LANES), lambda i: (j_layer, 0, 0)),
            pl.BlockSpec((None, 1, LANES), lambda i: (j_layer, 0, 0)),
        ],
        out_specs=[
            pl.BlockSpec((n // LANES, tm, LANES), lambda i: (0, i, 0)),
            pl.BlockSpec((tm, LANES), lambda i: (i, 0)),
        ],
        out_shape=[jax.ShapeDtypeStruct((n // LANES, m, LANES), F32), jax.ShapeDtypeStruct((m, LANES), F32)],
        compiler_params=_cp("arbitrary"),
        name="rec_in_proj",
    )(x, g_all, mod.arr, mod.arr, w_all, wg_all, bg_all)


def _head_rmsnorm_rows(a, gain, gmat):
    ms = _dot(_hilo(a * a), gmat)
    return a * lax.rsqrt(ms + EPS) * gain


def _write_normed_rows(o_ref, a, gain_ref, gm_ref):
    for c in range(MOBA_COLS // LANES):
        sl = slice(c * LANES, (c + 1) * LANES)
        o_ref[:, sl] = _head_rmsnorm_rows(a[:, sl], gain_ref[...], gm_ref[...])
    o_ref[:, MOBA_COLS:] = a[:, MOBA_COLS:]


def _qkv_kernel(x_ref, g_ref, sc_ref, sh_ref, wq_ref, wkt_ref, wvt_ref, qg_ref, kgc_ref, gm_ref, *rest, rows):
    if rows:
        wk_ref, wv_ref, kg_ref, q_ref, kt_ref, vt_ref, k_ref, v_ref = rest
    else:
        q_ref, kt_ref, vt_ref = rest
    h = _norm_mod(x_ref, g_ref, sc_ref, sh_ref)
    tm = h.shape[0]
    _write_normed_rows(q_ref, _dot(h, wq_ref[...]), qg_ref, gm_ref)
    kt = _dot_nt(wkt_ref[...], h)
    a = kt[:MOBA_COLS].reshape(N_MOBA_HEADS, HEAD_DIM, tm)
    ms = jnp.mean(a * a, axis=1, keepdims=True)
    kt_ref[:MOBA_COLS, :] = (a * lax.rsqrt(ms + EPS) * kgc_ref[...].reshape(1, HEAD_DIM, 1)).reshape(MOBA_COLS, tm)
    kt_ref[MOBA_COLS:, :] = kt[MOBA_COLS:]
    vt_ref[...] = _dot_nt(wvt_ref[...], h)
    if rows:
        _write_normed_rows(k_ref, _dot(h, wk_ref[...]), kg_ref, gm_ref)
        v_ref[...] = _dot(h, wv_ref[...])


def _group_mean_matrix():
    r = jnp.arange(2 * LANES) % LANES
    c = jnp.arange(LANES)
    return jnp.where((r[:, None] // HEAD_DIM) == (c[None, :] // HEAD_DIM), 1.0 / HEAD_DIM, 0.0).astype(BF16)


def _qkv_proj(x, g_all, mod, w, j_layer, rows_per_batch, rows):
    m, d = x.shape
    hd = N_ATT_HEADS * HEAD_DIM
    tm, l = mod.tm, mod.l
    seq = m if rows_per_batch == 1 else rows_per_batch
    batch = m // seq
    bps = seq // tm
    wspec = pl.BlockSpec((None, d, hd), lambda i: (j_layer, 0, 0))
    wtspec = pl.BlockSpec((None, hd, d), lambda i: (j_layer, 0, 0))
    gspec = pl.BlockSpec((None, 1, LANES), lambda i: (j_layer, 0, 0))
    in_specs = [
        pl.BlockSpec((tm, d), lambda i: (i, 0)),
        pl.BlockSpec((None, 1, d), lambda i: (l, 0, 0)),
        mod.spec(1),
        mod.spec(0),
        wspec, wtspec, wtspec, gspec,
        pl.BlockSpec((None, HEAD_DIM, 1), lambda i: (j_layer, 0, 0)),
        pl.BlockSpec((2 * LANES, LANES), lambda i: (0, 0)),
    ]
    args = [x, g_all, mod.arr, mod.arr, w["att_wq"], w["att_wkt"], w["att_wvt"], w["att_qn_g"], w["att_kn_gc"],
            _group_mean_matrix()]
    row_out = jax.ShapeDtypeStruct((m, hd), F32)
    t_out = jax.ShapeDtypeStruct((batch, hd, seq), F32)
    row_spec = pl.BlockSpec((tm, hd), lambda i: (i, 0))
    t_spec = pl.BlockSpec((None, hd, tm), lambda i: (i // bps, 0, i % bps))
    out_shape, out_specs = [row_out, t_out, t_out], [row_spec, t_spec, t_spec]
    if rows:
        in_specs += [wspec, wspec, gspec]
        args += [w["att_wk"], w["att_wv"], w["att_kn_g"]]
        out_shape += [row_out, row_out]
        out_specs += [row_spec, row_spec]
    return pl.pallas_call(
        functools.partial(_qkv_kernel, rows=rows),
        grid=(m // tm,),
        in_specs=in_specs,
        out_specs=out_specs,
        out_shape=out_shape,
        compiler_params=_cp("arbitrary"),
        name="qkv_proj",
    )(*args)


def _res_mm_kernel(a_ref, w_ref, r_ref, gate_ref, o_ref):
    o_ref[...] = r_ref[...] + gate_ref[...] * _dot(a_ref[...], w_ref[...])


def _res_matmul(a, w_all, l_w, res, mod, gate_chunk):
    m, kdim = a.shape
    n = w_all.shape[-1]
    tm = mod.tm
    return pl.pallas_call(
        _res_mm_kernel,
        grid=(m // tm,),
        in_specs=[
            pl.BlockSpec((tm, kdim), lambda i: (i, 0)),
            pl.BlockSpec((None, kdim, n), lambda i: (l_w, 0, 0)),
            pl.BlockSpec((tm, n), lambda i: (i, 0)),
            mod.spec(gate_chunk),
        ],
        out_specs=pl.BlockSpec((tm, n), lambda i: (i, 0)),
        out_shape=jax.ShapeDtypeStruct((m, n), F32),
        compiler_params=_cp("arbitrary"),
        name="res_matmul",
    )(a, w_all, res, mod.arr)


def _moba_prompt_kernel(q_ref, kt_ref, vt_ref, slope_ref, o_ref, kmean_ref, *, nb):
    qi = pl.program_id(2)
    blk = MOBA_BLOCK
    scale = HEAD_DIM ** -0.5
    lane = _iota((1, LANES), 1)
    row = _iota((blk, blk), 0)
    col = _iota((blk, blk), 1)
    blk_id = _iota((blk, LANES), 1)
    blk_lane = _iota((LANES, LANES), 1)
    q = q_ref[...]

    @pl.when(qi == 0)
    def _():
        kmean_t = jnp.zeros((LANES, LANES), F32)
        for n in range(nb):
            mean_n = jnp.mean(kt_ref[:, n * blk:(n + 1) * blk], axis=-1, keepdims=True)
            kmean_t = jnp.where(blk_lane == n, mean_n, kmean_t)
        kmean_ref[...] = kmean_t

    kmean_t = kmean_ref[...]
    heads = []
    for h2 in range(2):
        hmask = (lane < HEAD_DIM) if h2 == 0 else (lane >= HEAD_DIM)
        slope = slope_ref[:, h2 * HEAD_DIM:h2 * HEAD_DIM + 1]
        qf = jnp.where(hmask, q, 0.0)
        gate = jnp.dot(qf, kmean_t, precision=lax.Precision.HIGHEST, preferred_element_type=F32)
        heads.append((slope, (qf * scale).astype(BF16), jnp.where(blk_id < qi, gate, -jnp.inf)))

    def block(n, carries, own):
        start = pl.multiple_of(n * blk, blk)
        ktn = kt_ref[:, pl.ds(start, blk)].astype(BF16)
        vtn = vt_ref[:, pl.ds(start, blk)].astype(BF16)
        dist = ((qi - n) * blk + (row - col)).astype(F32)
        new = []
        for (slope, qh, gate), (m, l, acc) in zip(heads, carries):
            s = _dot(qh, ktn) - slope * dist
            if own:
                keep = col <= row
            else:
                g_n = jnp.sum(jnp.where(blk_id == n, gate, 0.0), axis=-1, keepdims=True)
                beats = (gate > g_n) | ((gate == g_n) & (blk_id < n))
                keep = jnp.sum(beats.astype(F32), axis=-1, keepdims=True) < MOBA_TOPK
            s = jnp.where(keep, s, NEG)
            m_new = jnp.maximum(m, jnp.max(s, axis=-1, keepdims=True))
            alpha = jnp.exp(m - m_new)
            p = jnp.exp(s - m_new)
            new.append((m_new, alpha * l + jnp.sum(p, axis=-1, keepdims=True),
                        alpha * acc + _dot_nt(p.astype(BF16), vtn)))
        return tuple(new)

    init = (jnp.full((blk, 1), NEG, F32), jnp.zeros((blk, 1), F32), jnp.zeros((blk, LANES), F32))
    carries = lax.fori_loop(0, qi, lambda n, c: block(n, c, False), (init, init))
    (_, l0, acc0), (_, l1, acc1) = block(qi, carries, True)
    o_ref[...] = jnp.where(lane < HEAD_DIM, acc0 / l0, acc1 / l1).astype(o_ref.dtype)


def _alibi_table():
    t = jnp.asarray(_alibi_slopes(), F32).reshape(N_MOBA_HEADS // 2, 2, 1)
    return jnp.broadcast_to(t, (N_MOBA_HEADS // 2, 2, HEAD_DIM)).reshape(N_MOBA_HEADS // 2, 1, LANES)


def _moba_prompt(q, kt, vt, batch, seq):
    nb = seq // MOBA_BLOCK
    npair = N_MOBA_HEADS // 2
    return pl.pallas_call(
        functools.partial(_moba_prompt_kernel, nb=nb),
        grid=(batch, npair, nb),
        in_specs=[
            pl.BlockSpec((MOBA_BLOCK, LANES), lambda b, hp, qi: (b * nb + qi, hp)),
            pl.BlockSpec((None, LANES, seq), lambda b, hp, qi: (b, hp, 0)),
            pl.BlockSpec((None, LANES, seq), lambda b, hp, qi: (b, hp, 0)),
            pl.BlockSpec((None, 1, LANES), lambda b, hp, qi: (hp, 0, 0)),
        ],
        out_specs=pl.BlockSpec((MOBA_BLOCK, LANES), lambda b, hp, qi: (b * nb + qi, hp)),
        out_shape=jax.ShapeDtypeStruct((batch * seq, MOBA_COLS), BF16),
        scratch_shapes=[pltpu.VMEM((LANES, LANES), F32)],
        compiler_params=_cp("arbitrary", "arbitrary", "arbitrary"),
        name="moba_prompt",
    )(q, kt, vt, _alibi_table())


SB_BLOCK = 256


def _later_matrix(n):
    j = jnp.arange(2 * n) % n
    s = jnp.arange(n)
    return (j[:, None] > s[None, :]).astype(BF16)


def _sb_prompt_kernel(q_ref, kt_ref, vt_ref, u_ref, o_ref):
    qi = pl.program_id(2)
    blk = SB_BLOCK
    scale = HEAD_DIM ** -0.5
    lane = _iota((1, LANES), 1)
    row = _iota((2 * blk, blk), 0) % blk
    col = _iota((2 * blk, blk), 1)
    q = q_ref[...] * scale
    qh = jnp.concatenate([jnp.where(lane < HEAD_DIM, q, 0.0), jnp.where(lane >= HEAD_DIM, q, 0.0)],
                         axis=0).astype(BF16)

    def block(n, state, mask):
        carry, acc = state
        start = pl.multiple_of(n * blk, blk)
        ktn = kt_ref[:, pl.ds(start, blk)].astype(BF16)
        vtn = vt_ref[:, pl.ds(start, blk)].astype(BF16)
        z = _dot(qh, ktn)
        tail = _softplus_tail(z)
        lk = -jnp.maximum(z, 0.0) - tail
        ls = jnp.minimum(z, 0.0) - tail
        if mask is not None:
            lk = jnp.where(mask, lk, 0.0)
        later = _dot(_hilo(lk), u_ref[...]) + carry
        w = jnp.exp(ls + later)
        if mask is not None:
            w = jnp.where(mask, w, 0.0)
        return carry + jnp.sum(lk, axis=-1, keepdims=True), acc + _dot_nt(w.astype(BF16), vtn)

    state = block(qi, (jnp.zeros((2 * blk, 1), F32), jnp.zeros((2 * blk, LANES), F32)), col < row)
    _, acc = lax.fori_loop(0, qi, lambda r, st: block(qi - 1 - r, st, None), state)
    o_ref[...] = jnp.where(lane < HEAD_DIM, acc[:blk], acc[blk:]).astype(o_ref.dtype)


def _sb_prompt(q, kt, vt, batch, seq):
    nq = seq // SB_BLOCK
    npair = N_SB_HEADS // 2
    off = N_MOBA_HEADS // 2
    return pl.pallas_call(
        _sb_prompt_kernel,
        grid=(batch, npair, nq),
        in_specs=[
            pl.BlockSpec((SB_BLOCK, LANES), lambda b, hp, qi: (b * nq + qi, off + hp)),
            pl.BlockSpec((None, LANES, seq), lambda b, hp, qi: (b, off + hp, 0)),
            pl.BlockSpec((None, LANES, seq), lambda b, hp, qi: (b, off + hp, 0)),
            pl.BlockSpec((2 * SB_BLOCK, SB_BLOCK), lambda b, hp, qi: (0, 0)),
        ],
        out_specs=pl.BlockSpec((SB_BLOCK, LANES), lambda b, hp, qi: (b * nq + qi, hp)),
        out_shape=jax.ShapeDtypeStruct((batch * seq, N_SB_HEADS * HEAD_DIM), BF16),
        compiler_params=_cp("arbitrary", "arbitrary", "arbitrary"),
        name="sb_prompt",
    )(q, kt, vt, _later_matrix(SB_BLOCK))


def _head_columns(g, first):
    return jnp.stack([g[:, first + h:first + h + 1] for h in range(N_REC_HEADS)])


def _mlstm_prompt_kernel(q_ref, k_ref, v_ref, o_ref, gate_ref, hn_ref, hh_ref, c_ref, n_ref, m_ref):
    lc = REC_CHUNK
    dh = LANES
    hds = N_REC_HEADS

    @pl.when(pl.program_id(1) == 0)
    def _():
        c_ref[...] = jnp.zeros_like(c_ref)
        n_ref[...] = jnp.zeros_like(n_ref)
        m_ref[...] = jnp.zeros_like(m_ref)

    ri = _iota((1, lc, lc), 1)
    ci = _iota((1, lc, lc), 2)
    eye = ri == ci
    tril = ci <= ri
    q3 = q_ref[...]
    k3 = k_ref[...] * (dh ** -0.5)
    v3 = v_ref[...]
    g = gate_ref[...]
    i_col = _head_columns(g, 0)
    lf_col = _log_sigmoid(_head_columns(g, hds))
    c_old = c_ref[...]
    n_old = n_ref[...]
    m_old = m_ref[...]
    qb, kb, vb = q3.astype(BF16), k3.astype(BF16), v3.astype(BF16)
    cb = c_old.astype(BF16)

    lf_row = jnp.sum(jnp.where(eye, lf_col, 0.0), axis=1, keepdims=True)
    i_row = jnp.sum(jnp.where(eye, i_col, 0.0), axis=1, keepdims=True)
    b_col = jnp.sum(jnp.where(tril, lf_row, 0.0), axis=2, keepdims=True)
    b_row = jnp.sum(jnp.where(ri <= ci, lf_col, 0.0), axis=1, keepdims=True)
    a_row = i_row - b_row
    a_col = i_col - b_col
    cm_col = jnp.max(jnp.where(tril, a_row, -jnp.inf), axis=2, keepdims=True)
    mt_col = b_col + jnp.maximum(m_old, cm_col)
    inter = jnp.exp(m_old + b_col - mt_col)
    dmat = jnp.where(tril, jnp.exp(jnp.where(tril, a_row + (b_col - mt_col), 0.0)), 0.0)
    s = dmat * jnp.stack([_dot_nt(qb[h], kb[h]) for h in range(hds)])
    sb = s.astype(BF16)
    num = inter * jnp.stack([_dot(qb[h], cb[h]) for h in range(hds)]) \
        + jnp.stack([_dot(sb[h], vb[h]) for h in range(hds)])
    den = inter * jnp.sum(q3 * n_old, axis=2, keepdims=True) + jnp.sum(s, axis=2, keepdims=True)
    hh = num / jnp.maximum(jnp.abs(den), jnp.exp(-mt_col))
    b_last = b_col[:, lc - 1:lc, :]
    m_new = mt_col[:, lc - 1:lc, :]
    w_last = jnp.exp(a_col + (b_last - m_new))
    decay = jnp.exp(m_old + b_last - m_new)
    kw = k3 * w_last
    kwb = kw.astype(BF16)
    c_ref[...] = decay * c_old + jnp.stack([_dot_tn(kwb[h], vb[h]) for h in range(hds)])
    n_ref[...] = decay * n_old + jnp.sum(kw, axis=1, keepdims=True)
    m_ref[...] = m_new

    y = hh * lax.rsqrt(jnp.mean(hh * hh, axis=2, keepdims=True) + EPS) * hn_ref[...]
    out = (y * _sigmoid(o_ref[...])).astype(hh_ref.dtype)
    for h in range(hds):
        hh_ref[:, h * dh:(h + 1) * dh] = out[h]


def _mlstm_prompt(proj, gates, hn_g, batch, seq):
    lc = REC_CHUNK
    nc = seq // lc
    hds, dh = N_REC_HEADS, LANES
    d = hds * dh

    def part(kind):
        return pl.BlockSpec((hds, lc, dh), lambda b, c: (kind, b * nc + c, 0))

    return pl.pallas_call(
        _mlstm_prompt_kernel,
        grid=(batch, nc),
        in_specs=[
            part(0), part(1), part(2), part(3),
            pl.BlockSpec((lc, LANES), lambda b, c: (b * nc + c, 0)),
            pl.BlockSpec((hds, 1, dh), lambda b, c: (0, 0, 0)),
        ],
        out_specs=[
            pl.BlockSpec((lc, d), lambda b, c: (b * nc + c, 0)),
            pl.BlockSpec((None, hds, dh, dh), lambda b, c: (b, 0, 0, 0)),
            pl.BlockSpec((None, hds, 1, dh), lambda b, c: (b, 0, 0, 0)),
            pl.BlockSpec((None, hds, 1, 1), lambda b, c: (b, 0, 0, 0)),
        ],
        out_shape=[
            jax.ShapeDtypeStruct((batch * seq, d), BF16),
            jax.ShapeDtypeStruct((batch, hds, dh, dh), F32),
            jax.ShapeDtypeStruct((batch, hds, 1, dh), F32),
            jax.ShapeDtypeStruct((batch, hds, 1, 1), F32),
        ],
        compiler_params=_cp("arbitrary", "arbitrary"),
        name="mlstm_prompt",
    )(proj, proj, proj, proj, gates, hn_g.reshape(hds, 1, dh))


REC_STEP_BATCH = 8


def _mlstm_step_kernel(p_ref, gate_ref, hn_ref, c0_ref, n0_ref, m0_ref, hh_ref, c_ref, n_ref, m_ref):
    dh = LANES
    hds = N_REC_HEADS
    eye = _iota((1, dh, dh), 1) == _iota((1, dh, dh), 2)
    hn = hn_ref[...]
    for bb in range(REC_STEP_BATCH):
        row = slice(bb, bb + 1)
        q3 = p_ref[0:hds, row, :]
        k3 = p_ref[hds:2 * hds, row, :] * (dh ** -0.5)
        v3 = p_ref[2 * hds:3 * hds, row, :]
        o3 = p_ref[3 * hds:4 * hds, row, :]
        g = gate_ref[row, :]
        i_pre = _head_columns(g, 0)
        lf = _log_sigmoid(_head_columns(g, hds))
        m_old = m0_ref[bb]
        n_old = n0_ref[bb]
        c_old = c0_ref[bb]
        mt = lf + jnp.maximum(m_old, i_pre - lf)
        inter = jnp.exp(m_old + lf - mt)
        w_last = jnp.exp(i_pre - mt)
        q_col = jnp.sum(jnp.where(eye, q3, 0.0), axis=2, keepdims=True)
        k_col = jnp.sum(jnp.where(eye, k3, 0.0), axis=2, keepdims=True)
        qc = jnp.sum(q_col * c_old, axis=1, keepdims=True)
        qk = jnp.sum(q3 * k3, axis=2, keepdims=True)
        qn = jnp.sum(q3 * n_old, axis=2, keepdims=True)
        s = w_last * qk
        num = inter * qc + s * v3
        den = inter * qn + s
        hh = num / jnp.maximum(jnp.abs(den), jnp.exp(-mt))
        c_ref[bb] = inter * c_old + (w_last * k_col) * v3
        n_ref[bb] = inter * n_old + w_last * k3
        m_ref[bb] = mt
        y = hh * lax.rsqrt(jnp.mean(hh * hh, axis=2, keepdims=True) + EPS) * hn
        out = (y * _sigmoid(o3)).astype(hh_ref.dtype)
        for h in range(hds):
            hh_ref[row, h * dh:(h + 1) * dh] = out[h]


def _mlstm_step(proj, gates, hn_g, state_c, state_n, state_m, j_layer):
    nb = proj.shape[1]
    hds, dh = N_REC_HEADS, LANES
    d = hds * dh
    bb = REC_STEP_BATCH
    return pl.pallas_call(
        _mlstm_step_kernel,
        grid=(nb // bb,),
        in_specs=[
            pl.BlockSpec((4 * hds, bb, dh), lambda i: (0, i, 0)),
            pl.BlockSpec((bb, LANES), lambda i: (i, 0)),
            pl.BlockSpec((hds, 1, dh), lambda i: (0, 0, 0)),
            pl.BlockSpec((None, bb, hds, dh, dh), lambda i: (j_layer, i, 0, 0, 0)),
            pl.BlockSpec((None, bb, hds, 1, dh), lambda i: (j_layer, i, 0, 0, 0)),
            pl.BlockSpec((None, bb, hds, 1, 1), lambda i: (j_layer, i, 0, 0, 0)),
        ],
        out_specs=[
            pl.BlockSpec((bb, d), lambda i: (i, 0)),
            pl.BlockSpec((bb, hds, dh, dh), lambda i: (i, 0, 0, 0)),
            pl.BlockSpec((bb, hds, 1, dh), lambda i: (i, 0, 0, 0)),
            pl.BlockSpec((bb, hds, 1, 1), lambda i: (i, 0, 0, 0)),
        ],
        out_shape=[
            jax.ShapeDtypeStruct((nb, d), BF16),
            jax.ShapeDtypeStruct((nb, hds, dh, dh), F32),
            jax.ShapeDtypeStruct((nb, hds, 1, dh), F32),
            jax.ShapeDtypeStruct((nb, hds, 1, 1), F32),
        ],
        compiler_params=_cp("arbitrary"),
        name="mlstm_step",
    )(proj, gates, hn_g.reshape(hds, 1, dh), state_c, state_n, state_m)


PAGES_PER_STEP = 8


def _spread_heads(x):
    h, l = x.shape
    return jnp.broadcast_to(x[:, None, :], (h, HEAD_DIM, l)).reshape(h * HEAD_DIM, l)


def _row_to_column(r):
    eye = _iota((LANES, LANES), 0) == _iota((LANES, LANES), 1)
    return jnp.sum(jnp.where(eye, r, 0.0), axis=1, keepdims=True)


def _column_to_row(c):
    eye = _iota((LANES, LANES), 0) == _iota((LANES, LANES), 1)
    return jnp.sum(jnp.where(eye, c, 0.0), axis=0, keepdims=True)


def _decode_attn_kernel(pt_ref, q_ref, kn_ref, vn_ref, slope_ref, u_ref, *refs, page, n_pages, nblk):
    pp = PAGES_PER_STEP
    k_refs, v_refs = refs[:pp], refs[pp:2 * pp]
    o_ref = refs[2 * pp]
    q_bc, sb_acc, sb_carry, mb_m, mb_l, mb_acc, mb_gate = refs[2 * pp + 1:]
    g = pl.program_id(1)
    nsteps = n_pages // pp
    hm = MOBA_COLS
    hd = N_ATT_HEADS * HEAD_DIM
    scale = HEAD_DIM ** -0.5
    pos = n_pages * page
    pages_per_blk = MOBA_BLOCK // page
    blks_per_step = pp // pages_per_blk
    slope = slope_ref[...]
    lane_pos = _iota((1, page), 1)

    @pl.when(g == 0)
    def _():
        for c in range(hd // LANES):
            sl = slice(c * LANES, (c + 1) * LANES)
            q_bc[sl, :] = jnp.broadcast_to(_row_to_column(q_ref[:, sl]), (LANES, page))
        sb_acc[...] = jnp.zeros_like(sb_acc)
        sb_carry[...] = jnp.zeros_like(sb_carry)

    q_all = q_bc[...]
    raws, zs = [], []
    for i in range(pp):
        sc = jnp.sum((k_refs[i][...] * q_all).reshape(N_ATT_HEADS, HEAD_DIM, page), axis=1)
        raws.append(sc[:N_MOBA_HEADS])
        zs.append(sc[N_MOBA_HEADS:] * scale)

    tails = [_softplus_tail(z) for z in zs]
    lks = [-jnp.maximum(z, 0.0) - t for z, t in zip(zs, tails)]
    later_all = _dot(_hilo(jnp.concatenate(lks, axis=0)), u_ref[...])
    carry = sb_carry[:, 0:1]
    acc_sb = sb_acc[...]
    for i in range(pp):
        later = later_all[i * N_SB_HEADS:(i + 1) * N_SB_HEADS] + carry
        w = jnp.exp(jnp.minimum(zs[i], 0.0) - tails[i] + later)
        acc_sb = acc_sb + v_refs[i][hm:, :] * _spread_heads(w)
        carry = carry + jnp.sum(lks[i], axis=-1, keepdims=True)
    sb_acc[...] = acc_sb
    sb_carry[...] = jnp.broadcast_to(carry, (N_SB_HEADS, LANES))

    for half in range(blks_per_step):
        idx = [half * pages_per_blk + t for t in range(pages_per_blk)]
        s_parts = []
        for i in idx:
            pg = n_pages - 1 - (g * pp + i)
            dist = (pos - (pg * page + lane_pos)).astype(F32)
            s_parts.append(raws[i] * scale - slope * dist)
        gate_blk = functools.reduce(lambda a, b: a + b, [jnp.sum(raws[i], axis=-1, keepdims=True) for i in idx])
        m_blk = functools.reduce(jnp.maximum, [jnp.max(s, axis=-1, keepdims=True) for s in s_parts])
        l_blk = jnp.zeros((N_MOBA_HEADS, 1), F32)
        a_blk = jnp.zeros((hm, page), F32)
        for i, s in zip(idx, s_parts):
            p = jnp.exp(s - m_blk)
            l_blk = l_blk + jnp.sum(p, axis=-1, keepdims=True)
            a_blk = a_blk + v_refs[i][:hm, :] * _spread_heads(p)
        n_blk = nblk - 1 - (g * blks_per_step + half)
        mb_m[n_blk] = jnp.broadcast_to(m_blk, (N_MOBA_HEADS, LANES))
        mb_l[n_blk] = jnp.broadcast_to(l_blk, (N_MOBA_HEADS, LANES))
        mb_acc[n_blk] = a_blk
        mb_gate[n_blk] = jnp.broadcast_to(gate_blk * (1.0 / MOBA_BLOCK), (N_MOBA_HEADS, LANES))

    @pl.when(g == nsteps - 1)
    def _():
        gates = [mb_gate[n][:, 0:1] for n in range(nblk)]
        diag = _iota((N_MOBA_HEADS, hm), 0) == _iota((N_MOBA_HEADS, hm), 1) // HEAD_DIM
        q_mb = jnp.where(diag, q_ref[:, :hm], 0.0)
        s_self = jnp.sum(q_mb * kn_ref[:, :hm], axis=-1, keepdims=True) * scale
        sels, m_fin = [], s_self
        for n in range(nblk):
            rank = jnp.zeros((N_MOBA_HEADS, 1), F32)
            for mth in range(nblk):
                if mth == n:
                    continue
                beats = gates[mth] > gates[n]
                if mth < n:
                    beats = beats | (gates[mth] == gates[n])
                rank = rank + beats.astype(F32)
            sel = rank < MOBA_TOPK
            sels.append(sel)
            m_fin = jnp.maximum(m_fin, jnp.where(sel, mb_m[n][:, 0:1], NEG))
        p_self = jnp.exp(s_self - m_fin)
        l_fin = p_self
        a_fin = jnp.zeros((hm, page), F32)
        for n in range(nblk):
            wgt = jnp.where(sels[n], jnp.exp(mb_m[n][:, 0:1] - m_fin), 0.0)
            l_fin = l_fin + wgt * mb_l[n][:, 0:1]
            a_fin = a_fin + _spread_heads(wgt) * mb_acc[n]
        col_mb = jnp.sum(a_fin, axis=-1, keepdims=True)
        col_sb = jnp.sum(acc_sb, axis=-1, keepdims=True)
        w_self = _spread_heads(p_self)
        inv_l = _spread_heads(1.0 / l_fin)
        for c in range(hm // LANES):
            sl = slice(c * LANES, (c + 1) * LANES)
            v_col = _row_to_column(vn_ref[:, sl])
            o_ref[:, sl] = _column_to_row((col_mb[sl] + w_self[sl] * v_col) * inv_l[sl]).astype(o_ref.dtype)
            o_ref[:, hm + c * LANES:hm + (c + 1) * LANES] = _column_to_row(col_sb[sl]).astype(o_ref.dtype)


def _decode_attn(q, k_new, v_new, cache_kt, cache_vt, page_table, j_layer):
    nb = q.shape[0]
    n_pages = page_table.shape[1]
    hd, page = cache_kt.shape[2], cache_kt.shape[3]
    hm = MOBA_COLS
    pp = PAGES_PER_STEP
    nblk = n_pages * page // MOBA_BLOCK

    def page_spec(i):
        return pl.BlockSpec((None, None, hd, page),
                            lambda b, g, pt: (j_layer, pt[b * n_pages + (n_pages - 1 - (g * pp + i))], 0, 0))

    row = pl.BlockSpec((None, 1, hd), lambda b, g, pt: (b, 0, 0))
    grid_spec = pltpu.PrefetchScalarGridSpec(
        num_scalar_prefetch=1,
        grid=(nb, n_pages // pp),
        in_specs=[row, row, row,
                  pl.BlockSpec((N_MOBA_HEADS, 1), lambda b, g, pt: (0, 0)),
                  pl.BlockSpec((2 * page, page), lambda b, g, pt: (0, 0))]
                 + [page_spec(i) for i in range(pp)] + [page_spec(i) for i in range(pp)],
        out_specs=pl.BlockSpec((None, 1, hd), lambda b, g, pt: (b, 0, 0)),
        scratch_shapes=[
            pltpu.VMEM((hd, page), F32),
            pltpu.VMEM((hd - hm, page), F32),
            pltpu.VMEM((N_SB_HEADS, LANES), F32),
            pltpu.VMEM((nblk, N_MOBA_HEADS, LANES), F32),
            pltpu.VMEM((nblk, N_MOBA_HEADS, LANES), F32),
            pltpu.VMEM((nblk, hm, page), F32),
            pltpu.VMEM((nblk, N_MOBA_HEADS, LANES), F32),
        ],
    )
    out = pl.pallas_call(
        functools.partial(_decode_attn_kernel, page=page, n_pages=n_pages, nblk=nblk),
        grid_spec=grid_spec,
        out_shape=jax.ShapeDtypeStruct((nb, 1, hd), BF16),
        compiler_params=_cp("arbitrary", "arbitrary"),
        name="decode_attn",
    )(page_table.reshape(-1), q.reshape(nb, 1, hd), k_new.reshape(nb, 1, hd), v_new.reshape(nb, 1, hd),
      jnp.asarray(_alibi_slopes(), F32).reshape(N_MOBA_HEADS, 1), _later_matrix(page),
      *([cache_kt] * pp), *([cache_vt] * pp))
    return out.reshape(nb, hd)


ROW_TILE = 512


def _decoder(x, mod_arr, rows_per_batch, w, cache=None, state=None):
    m, d = x.shape
    depth = w["mlp_w1"].shape[0]
    batch = m // rows_per_batch
    tm = m if rows_per_batch == 1 else min(ROW_TILE, rows_per_batch)
    kts, vts, cs, ns, ms = [], [], [], [], []
    for l in range(depth):
        mod = _Mod(mod_arr, l, d, tm, rows_per_batch)
        j = l // 2
        if l % 2 == 0:
            if cache is None:
                q, kt, vt = _qkv_proj(x, w["norm1_g"], mod, w, j, rows_per_batch, rows=False)
                oa = _moba_prompt(q, kt, vt, batch, rows_per_batch)
                ob = _sb_prompt(q, kt, vt, batch, rows_per_batch)
                o = jnp.concatenate([oa, ob], axis=1)
            else:
                q, kt, vt, k, v = _qkv_proj(x, w["norm1_g"], mod, w, j, rows_per_batch, rows=True)
                o = _decode_attn(q, k, v, cache[0], cache[1], cache[2], j)
            kts.append(kt)
            vts.append(vt)
            x = _res_matmul(o, w["att_w_out"], j, x, mod, 2)
        else:
            proj, gates = _rec_in_proj(x, w["norm1_g"], mod, w["rec_w_main"], w["rec_w_gate"], w["rec_b_gate"], j)
            if state is None:
                hh, c_new, n_new, m_new = _mlstm_prompt(proj, gates, w["rec_hn_g"][j], batch, rows_per_batch)
            else:
                hh, c_new, n_new, m_new = _mlstm_step(proj, gates, w["rec_hn_g"][j], state[0], state[1], state[2], j)
            cs.append(c_new)
            ns.append(n_new.reshape(batch, N_REC_HEADS, LANES))
            ms.append(m_new.reshape(batch, N_REC_HEADS))
            x = _res_matmul(hh, w["rec_w_out"], j, x, mod, 2)
        hid = _nm_matmul(x, w["norm2_g"], mod, (4, 3), w["mlp_w1"], l, relu2=True, out_dtype=BF16)
        x = _res_matmul(hid, w["mlp_w2"], l, x, mod, 5)
    return x, jnp.stack(kts), jnp.stack(vts), jnp.stack(cs), jnp.stack(ns), jnp.stack(ms)


def _rows_from_feature_major(t):
    na, b, _, s = t.shape
    return jnp.transpose(t.reshape(na, b, N_ATT_HEADS, HEAD_DIM, s), (0, 1, 4, 2, 3))


def kernel(x_prompt, x_sample, cache_k, cache_v, page_table, state_C, state_n, state_m, c_prompt, c_sample,
           ada_w, ada_b, norm1_g, norm2_g, att_w_in, att_qn_g, att_kn_g, att_w_out,
           rec_w_in, rec_b_i, rec_b_f, rec_hn_g, rec_w_out, mlp_w1, mlp_w2):
    bp, seq, d = x_prompt.shape
    bs = x_sample.shape[0]
    depth = ada_w.shape[0]
    hds = N_REC_HEADS
    n_rec = rec_w_in.shape[0]
    hd = N_ATT_HEADS * HEAD_DIM

    mod = _adaln_mod(jnp.concatenate([c_prompt, c_sample], axis=0), ada_w, ada_b)
    mod_p = mod[:, :bp].reshape(depth, bp, 1, 6 * d)
    mod_s = mod[:, bp:]

    gate_w = jnp.pad(rec_w_in[:, :, 4 * d:], ((0, 0), (0, 0), (0, LANES - 2 * hds)))
    gate_b = jnp.pad(jnp.concatenate([rec_b_i, rec_b_f], axis=1), ((0, 0), (0, LANES - 2 * hds)))
    w_in16 = att_w_in.astype(BF16)
    w = {
        "norm1_g": norm1_g.reshape(depth, 1, d),
        "norm2_g": norm2_g.reshape(depth, 1, d),
        "att_wq": w_in16[:, :, :hd],
        "att_wk": w_in16[:, :, hd:2 * hd],
        "att_wv": w_in16[:, :, 2 * hd:],
        "att_wkt": jnp.transpose(w_in16[:, :, hd:2 * hd], (0, 2, 1)),
        "att_wvt": jnp.transpose(w_in16[:, :, 2 * hd:], (0, 2, 1)),
        "att_qn_g": jnp.tile(att_qn_g, (1, LANES // HEAD_DIM)).reshape(-1, 1, LANES),
        "att_kn_g": jnp.tile(att_kn_g, (1, LANES // HEAD_DIM)).reshape(-1, 1, LANES),
        "att_kn_gc": att_kn_g.reshape(-1, HEAD_DIM, 1),
        "att_w_out": att_w_out.astype(BF16),
        "rec_w_main": rec_w_in[:, :, :4 * d].astype(BF16),
        "rec_w_gate": gate_w.astype(BF16),
        "rec_b_gate": gate_b.reshape(n_rec, 1, LANES),
        "rec_hn_g": rec_hn_g,
        "rec_w_out": rec_w_out.astype(BF16),
        "mlp_w1": mlp_w1.astype(BF16),
        "mlp_w2": mlp_w2.astype(BF16),
    }

    yp, ktp, vtp, cp_, np_, mp = _decoder(x_prompt.reshape(bp * seq, d), mod_p, seq, w)

    def feature_major_cache(c):
        nl, pool, page = c.shape[:3]
        return jnp.transpose(c, (0, 1, 3, 4, 2)).reshape(nl, pool, hd, page)

    ys, kts, vts, csm, nsm, msm = _decoder(x_sample.reshape(bs, d), mod_s, 1, w,
                                           cache=(feature_major_cache(cache_k), feature_major_cache(cache_v),
                                                  page_table),
                                           state=(state_C, state_n.reshape(n_rec, bs, hds, 1, LANES),
                                                  state_m.reshape(n_rec, bs, hds, 1, 1)))
    k_s = jnp.transpose(_rows_from_feature_major(kts), (0, 2, 1, 3, 4))
    v_s = jnp.transpose(_rows_from_feature_major(vts), (0, 2, 1, 3, 4))
    return (yp.reshape(bp, seq, d), ys.reshape(bs, 1, d),
            _rows_from_feature_major(ktp), _rows_from_feature_major(vtp), k_s, v_s,
            cp_, np_, mp, csm, nsm, msm)
```

```python
import functools

import jax
import jax.numpy as jnp
from jax import lax
from jax.experimental import pallas as pl
from jax.experimental.pallas import tpu as pltpu

F32 = jnp.float32
BF16 = jnp.bfloat16

EPS = 1e-6
HEAD_DIM = 64
N_MOBA_HEADS = 8
N_SB_HEADS = 8
N_ATT_HEADS = N_MOBA_HEADS + N_SB_HEADS
MOBA_BLOCK = 256
MOBA_TOPK = 3
N_REC_HEADS = 8
REC_CHUNK = 64
LANES = 128
NEG = -1e30
VMEM_LIMIT = 56 * 1024 * 1024
MOBA_COLS = N_MOBA_HEADS * HEAD_DIM


def _cp(*sem):
    return pltpu.CompilerParams(dimension_semantics=sem, vmem_limit_bytes=VMEM_LIMIT)


def _sigmoid(x):
    return 1.0 / (1.0 + jnp.exp(-x))


def _softplus_tail(z):
    return jnp.log(1.0 + jnp.exp(-jnp.abs(z)))


def _log_sigmoid(z):
    return jnp.minimum(z, 0.0) - _softplus_tail(z)


def _dot(a, b):
    return jnp.dot(a, b, preferred_element_type=F32)


def _dot_nt(a, b):
    return lax.dot_general(a, b, (((1,), (1,)), ((), ())), preferred_element_type=F32)


def _dot_tn(a, b):
    return lax.dot_general(a, b, (((0,), (0,)), ((), ())), preferred_element_type=F32)


def _hilo(x):
    hi = x.astype(BF16)
    lo = (x - hi.astype(F32)).astype(BF16)
    return jnp.concatenate([hi, lo], axis=-1)


def _iota(shape, dim):
    return lax.broadcasted_iota(jnp.int32, shape, dim)


def _alibi_slopes():
    return [2.0 ** (-8.0 * (h + 1) / N_MOBA_HEADS) for h in range(N_MOBA_HEADS)]


def _mod_kernel(c_ref, w_ref, b_ref, o_ref):
    c = c_ref[...]
    s = c * _sigmoid(c)
    o_ref[...] = _dot(s.astype(BF16), w_ref[...].astype(BF16)) + b_ref[...]


def _adaln_mod(c_all, ada_w, ada_b):
    depth, d, n = ada_w.shape
    m = c_all.shape[0]
    tn = 1024
    return pl.pallas_call(
        _mod_kernel,
        grid=(depth, n // tn),
        in_specs=[
            pl.BlockSpec((m, d), lambda l, j: (0, 0)),
            pl.BlockSpec((None, d, tn), lambda l, j: (l, 0, j)),
            pl.BlockSpec((None, 1, tn), lambda l, j: (l, 0, j)),
        ],
        out_specs=pl.BlockSpec((None, m, tn), lambda l, j: (l, 0, j)),
        out_shape=jax.ShapeDtypeStruct((depth, m, n), F32),
        compiler_params=_cp("arbitrary", "arbitrary"),
        name="adaln_mod",
    )(c_all, ada_w, ada_b.reshape(depth, 1, n))


class _Mod:
    def __init__(self, arr, l, d, tm, rows_per_batch):
        self.arr, self.l, self.d, self.tm = arr, l, d, tm
        self.per_row = rows_per_batch == 1
        self.blocks_per_batch = None if self.per_row else rows_per_batch // tm

    def spec(self, chunk):
        d, l = self.d, self.l
        if self.per_row:
            return pl.BlockSpec((None, self.tm, d), lambda i: (l, i, chunk))
        bpb = self.blocks_per_batch
        return pl.BlockSpec((None, None, 1, d), lambda i: (l, i // bpb, 0, chunk))


def _norm_mod(x_ref, g_ref, sc_ref, sh_ref):
    x = x_ref[...]
    y = x * lax.rsqrt(jnp.mean(x * x, axis=-1, keepdims=True) + EPS) * g_ref[...]
    return (y * (1.0 + sc_ref[...]) + sh_ref[...]).astype(BF16)


COL_CHUNK = 1024


def _nm_mm_kernel(x_ref, g_ref, sc_ref, sh_ref, w_ref, o_ref, *, relu2):
    h = _norm_mod(x_ref, g_ref, sc_ref, sh_ref)
    for c in range(w_ref.shape[1] // COL_CHUNK):
        sl = slice(c * COL_CHUNK, (c + 1) * COL_CHUNK)
        acc = _dot(h, w_ref[:, sl])
        if relu2:
            acc = jnp.square(jnp.maximum(acc, 0.0))
        o_ref[:, sl] = acc.astype(o_ref.dtype)


def _nm_matmul(x, g_all, mod, chunks, w_all, l_w, *, relu2, out_dtype):
    m, d = x.shape
    n = w_all.shape[-1]
    tm, l = mod.tm, mod.l
    return pl.pallas_call(
        functools.partial(_nm_mm_kernel, relu2=relu2),
        grid=(m // tm,),
        in_specs=[
            pl.BlockSpec((tm, d), lambda i: (i, 0)),
            pl.BlockSpec((None, 1, d), lambda i: (l, 0, 0)),
            mod.spec(chunks[0]),
            mod.spec(chunks[1]),
            pl.BlockSpec((None, d, n), lambda i: (l_w, 0, 0)),
        ],
        out_specs=pl.BlockSpec((tm, n), lambda i: (i, 0)),
        out_shape=jax.ShapeDtypeStruct((m, n), out_dtype),
        compiler_params=_cp("arbitrary"),
        name="nm_matmul",
    )(x, g_all, mod.arr, mod.arr, w_all)


def _rec_in_kernel(x_ref, g_ref, sc_ref, sh_ref, w_ref, wg_ref, bg_ref, o_ref, gate_ref):
    h = _norm_mod(x_ref, g_ref, sc_ref, sh_ref)
    gate_ref[...] = _dot(h, wg_ref[...]) + bg_ref[...]
    dm = N_REC_HEADS * LANES
    for kind in range(w_ref.shape[1] // dm):
        acc = _dot(h, w_ref[:, kind * dm:(kind + 1) * dm])
        for hh in range(N_REC_HEADS):
            o_ref[kind * N_REC_HEADS + hh] = acc[:, hh * LANES:(hh + 1) * LANES]


def _rec_in_proj(x, g_all, mod, w_all, wg_all, bg_all, j_layer):
    m, d = x.shape
    n = w_all.shape[-1]
    tm, l = mod.tm, mod.l
    return pl.pallas_call(
        _rec_in_kernel,
        grid=(m // tm,),
        in_specs=[
            pl.BlockSpec((tm, d), lambda i: (i, 0)),
            pl.BlockSpec((None, 1, d), lambda i: (l, 0, 0)),
            mod.spec(1),
            mod.spec(0),
            pl.BlockSpec((None, d, n), lambda i: (j_layer, 0, 0)),
            pl.BlockSpec((None, d, LANES), lambda i: (j_layer, 0, 0)),
            pl.BlockSpec((None, 1, LANES), lambda i: (j_layer, 0, 0)),
        ],
        out_specs=[
            pl.BlockSpec((n // LANES, tm, LANES), lambda i: (0, i, 0)),
            pl.BlockSpec((tm, LANES), lambda i: (i, 0)),
        ],
        out_shape=[jax.ShapeDtypeStruct((n // LANES, m, LANES), F32), jax.ShapeDtypeStruct((m, LANES), F32)],
        compiler_params=_cp("arbitrary"),
        name="rec_in_proj",
    )(x, g_all, mod.arr, mod.arr, w_all, wg_all, bg_all)


def _head_rmsnorm_rows(a, gain, gmat):
    ms = _dot(_hilo(a * a), gmat)
    return a * lax.rsqrt(ms + EPS) * gain


def _write_normed_rows(o_ref, a, gain_ref, gm_ref):
    for c in range(MOBA_COLS // LANES):
        sl = slice(c * LANES, (c + 1) * LANES)
        o_ref[:, sl] = _head_rmsnorm_rows(a[:, sl], gain_ref[...], gm_ref[...])
    o_ref[:, MOBA_COLS:] = a[:, MOBA_COLS:]


def _qkv_kernel(x_ref, g_ref, sc_ref, sh_ref, wq_ref, wkt_ref, wvt_ref, qg_ref, kgc_ref, gm_ref, *rest, rows, n_prev):
    rest = list(rest)
    if n_prev:
        ktp_ref, vtp_ref = rest[:2]
        rest = rest[2:]
    if rows:
        wk_ref, wv_ref, kg_ref, q_ref, kt_ref, vt_ref, k_ref, v_ref = rest
    else:
        q_ref, kt_ref, vt_ref = rest
    if n_prev:
        kt_ref[:n_prev] = ktp_ref[...]
        vt_ref[:n_prev] = vtp_ref[...]
    h = _norm_mod(x_ref, g_ref, sc_ref, sh_ref)
    tm = h.shape[0]
    _write_normed_rows(q_ref, _dot(h, wq_ref[...]), qg_ref, gm_ref)
    kt = _dot_nt(wkt_ref[...], h)
    a = kt[:MOBA_COLS].reshape(N_MOBA_HEADS, HEAD_DIM, tm)
    ms = jnp.mean(a * a, axis=1, keepdims=True)
    kt_ref[n_prev, :MOBA_COLS, :] = (a * lax.rsqrt(ms + EPS)
                                     * kgc_ref[...].reshape(1, HEAD_DIM, 1)).reshape(MOBA_COLS, tm)
    kt_ref[n_prev, MOBA_COLS:, :] = kt[MOBA_COLS:]
    vt_ref[n_prev] = _dot_nt(wvt_ref[...], h)
    if rows:
        _write_normed_rows(k_ref, _dot(h, wk_ref[...]), kg_ref, gm_ref)
        v_ref[...] = _dot(h, wv_ref[...])


def _group_mean_matrix():
    r = jnp.arange(2 * LANES) % LANES
    c = jnp.arange(LANES)
    return jnp.where((r[:, None] // HEAD_DIM) == (c[None, :] // HEAD_DIM), 1.0 / HEAD_DIM, 0.0).astype(BF16)


def _qkv_proj(x, g_all, mod, w, j_layer, rows_per_batch, rows, prev):
    m, d = x.shape
    hd = N_ATT_HEADS * HEAD_DIM
    tm, l = mod.tm, mod.l
    seq = m if rows_per_batch == 1 else rows_per_batch
    batch = m // seq
    bps = seq // tm
    wspec = pl.BlockSpec((None, d, hd), lambda i: (j_layer, 0, 0))
    wtspec = pl.BlockSpec((None, hd, d), lambda i: (j_layer, 0, 0))
    gspec = pl.BlockSpec((None, 1, LANES), lambda i: (j_layer, 0, 0))
    in_specs = [
        pl.BlockSpec((tm, d), lambda i: (i, 0)),
        pl.BlockSpec((None, 1, d), lambda i: (l, 0, 0)),
        mod.spec(1),
        mod.spec(0),
        wspec, wtspec, wtspec, gspec,
        pl.BlockSpec((None, HEAD_DIM, 1), lambda i: (j_layer, 0, 0)),
        pl.BlockSpec((2 * LANES, LANES), lambda i: (0, 0)),
    ]
    args = [x, g_all, mod.arr, mod.arr, w["att_wq"], w["att_wkt"], w["att_wvt"], w["att_qn_g"], w["att_kn_gc"],
            _group_mean_matrix()]
    n_prev = j_layer
    row_out = jax.ShapeDtypeStruct((m, hd), F32)
    t_out = jax.ShapeDtypeStruct((n_prev + 1, batch, hd, seq), F32)
    row_spec = pl.BlockSpec((tm, hd), lambda i: (i, 0))
    t_spec = pl.BlockSpec((n_prev + 1, None, hd, tm), lambda i: (0, i // bps, 0, i % bps))
    out_shape, out_specs = [row_out, t_out, t_out], [row_spec, t_spec, t_spec]
    if n_prev:
        prev_spec = pl.BlockSpec((n_prev, None, hd, tm), lambda i: (0, i // bps, 0, i % bps))
        in_specs += [prev_spec, prev_spec]
        args += list(prev)
    if rows:
        in_specs += [wspec, wspec, gspec]
        args += [w["att_wk"], w["att_wv"], w["att_kn_g"]]
        out_shape += [row_out, row_out]
        out_specs += [row_spec, row_spec]
    return pl.pallas_call(
        functools.partial(_qkv_kernel, rows=rows, n_prev=n_prev),
        grid=(m // tm,),
        in_specs=in_specs,
        out_specs=out_specs,
        out_shape=out_shape,
        compiler_params=_cp("arbitrary"),
        name="qkv_proj",
    )(*args)


def _res_mm_kernel(a_ref, w_ref, r_ref, gate_ref, o_ref):
    o_ref[...] = r_ref[...] + gate_ref[...] * _dot(a_ref[...], w_ref[...])


def _res_matmul(a, w_all, l_w, res, mod, gate_chunk):
    m, kdim = a.shape
    n = w_all.shape[-1]
    tm = mod.tm
    return pl.pallas_call(
        _res_mm_kernel,
        grid=(m // tm,),
        in_specs=[
            pl.BlockSpec((tm, kdim), lambda i: (i, 0)),
            pl.BlockSpec((None, kdim, n), lambda i: (l_w, 0, 0)),
            pl.BlockSpec((tm, n), lambda i: (i, 0)),
            mod.spec(gate_chunk),
        ],
        out_specs=pl.BlockSpec((tm, n), lambda i: (i, 0)),
        out_shape=jax.ShapeDtypeStruct((m, n), F32),
        compiler_params=_cp("arbitrary"),
        name="res_matmul",
    )(a, w_all, res, mod.arr)


def _moba_prompt_kernel(q_ref, kt_ref, vt_ref, slope_ref, o_ref, kmean_ref, *, nb):
    qi = pl.program_id(2)
    blk = MOBA_BLOCK
    scale = HEAD_DIM ** -0.5
    lane = _iota((1, LANES), 1)
    row = _iota((blk, blk), 0)
    col = _iota((blk, blk), 1)
    blk_id = _iota((blk, LANES), 1)
    blk_lane = _iota((LANES, LANES), 1)
    q = q_ref[...]

    @pl.when(qi == 0)
    def _():
        kmean_t = jnp.zeros((LANES, LANES), F32)
        for n in range(nb):
            mean_n = jnp.mean(kt_ref[:, n * blk:(n + 1) * blk], axis=-1, keepdims=True)
            kmean_t = jnp.where(blk_lane == n, mean_n, kmean_t)
        kmean_ref[...] = kmean_t

    kmean_t = kmean_ref[...]
    heads = []
    for h2 in range(2):
        hmask = (lane < HEAD_DIM) if h2 == 0 else (lane >= HEAD_DIM)
        slope = slope_ref[:, h2 * HEAD_DIM:h2 * HEAD_DIM + 1]
        qf = jnp.where(hmask, q, 0.0)
        gate = jnp.dot(qf, kmean_t, precision=lax.Precision.HIGHEST, preferred_element_type=F32)
        heads.append((slope, slope * (row - col).astype(F32), (qf * scale).astype(BF16),
                      jnp.where(blk_id < qi, gate, -jnp.inf)))

    def block(n, carries, own):
        start = pl.multiple_of(n * blk, blk)
        ktn = kt_ref[:, pl.ds(start, blk)].astype(BF16)
        vtn = vt_ref[:, pl.ds(start, blk)].astype(BF16)
        blk_dist = ((qi - n) * blk).astype(F32)
        new = []
        for (slope, in_blk, qh, gate), (m, l, acc) in zip(heads, carries):
            s = _dot(qh, ktn) - in_blk - slope * blk_dist
            if own:
                keep = col <= row
            else:
                g_n = jnp.sum(jnp.where(blk_id == n, gate, 0.0), axis=-1, keepdims=True)
                beats = (gate > g_n) | ((gate == g_n) & (blk_id < n))
                keep = jnp.sum(beats.astype(F32), axis=-1, keepdims=True) < MOBA_TOPK
            s = jnp.where(keep, s, NEG)
            m_new = jnp.maximum(m, jnp.max(s, axis=-1, keepdims=True))
            alpha = jnp.exp(m - m_new)
            p = jnp.exp(s - m_new)
            new.append((m_new, alpha * l + jnp.sum(p, axis=-1, keepdims=True),
                        alpha * acc + _dot_nt(p.astype(BF16), vtn)))
        return tuple(new)

    init = (jnp.full((blk, 1), NEG, F32), jnp.zeros((blk, 1), F32), jnp.zeros((blk, LANES), F32))
    carries = lax.fori_loop(0, qi, lambda n, c: block(n, c, False), (init, init))
    (_, l0, acc0), (_, l1, acc1) = block(qi, carries, True)
    o_ref[...] = jnp.where(lane < HEAD_DIM, acc0 / l0, acc1 / l1).astype(o_ref.dtype)


def _alibi_table():
    t = jnp.asarray(_alibi_slopes(), F32).reshape(N_MOBA_HEADS // 2, 2, 1)
    return jnp.broadcast_to(t, (N_MOBA_HEADS // 2, 2, HEAD_DIM)).reshape(N_MOBA_HEADS // 2, 1, LANES)


def _moba_prompt(q, kt, vt, li, batch, seq):
    nb = seq // MOBA_BLOCK
    npair = N_MOBA_HEADS // 2
    return pl.pallas_call(
        functools.partial(_moba_prompt_kernel, nb=nb),
        grid=(batch, npair, nb),
        in_specs=[
            pl.BlockSpec((MOBA_BLOCK, LANES), lambda b, hp, qi: (b * nb + qi, hp)),
            pl.BlockSpec((None, None, LANES, seq), lambda b, hp, qi: (li, b, hp, 0)),
            pl.BlockSpec((None, None, LANES, seq), lambda b, hp, qi: (li, b, hp, 0)),
            pl.BlockSpec((None, 1, LANES), lambda b, hp, qi: (hp, 0, 0)),
        ],
        out_specs=pl.BlockSpec((MOBA_BLOCK, LANES), lambda b, hp, qi: (b * nb + qi, hp)),
        out_shape=jax.ShapeDtypeStruct((batch * seq, MOBA_COLS), BF16),
        scratch_shapes=[pltpu.VMEM((LANES, LANES), F32)],
        compiler_params=_cp("arbitrary", "arbitrary", "arbitrary"),
        name="moba_prompt",
    )(q, kt, vt, _alibi_table())


SB_BLOCK = 256


def _later_matrix(n):
    j = jnp.arange(2 * n) % n
    s = jnp.arange(n)
    return (j[:, None] > s[None, :]).astype(BF16)


def _sb_prompt_kernel(q_ref, kt_ref, vt_ref, u_ref, o_ref):
    qi = pl.program_id(2)
    blk = SB_BLOCK
    scale = HEAD_DIM ** -0.5
    lane = _iota((1, LANES), 1)
    row = _iota((2 * blk, blk), 0) % blk
    col = _iota((2 * blk, blk), 1)
    q = q_ref[...] * scale
    qh = jnp.concatenate([jnp.where(lane < HEAD_DIM, q, 0.0), jnp.where(lane >= HEAD_DIM, q, 0.0)],
                         axis=0).astype(BF16)

    def scores(n):
        return _dot(qh, kt_ref[:, pl.ds(pl.multiple_of(n * blk, blk), blk)].astype(BF16))

    def weights(z, carry, mask):
        sp = jnp.maximum(z, 0.0) + _softplus_tail(z)
        ls = z - sp
        if mask is not None:
            sp = jnp.where(mask, sp, 0.0)
        later = _dot(_hilo(sp), u_ref[...]) + carry
        w = jnp.exp(ls - later)
        if mask is not None:
            w = jnp.where(mask, w, 0.0)
        return w.astype(BF16), carry + jnp.sum(sp, axis=-1, keepdims=True)

    def weighted_values(w, n):
        return _dot_nt(w, vt_ref[:, pl.ds(pl.multiple_of(n * blk, blk), blk)].astype(BF16))

    w, carry = weights(scores(qi), jnp.zeros((2 * blk, 1), F32), col < row)
    z_next = scores(jnp.maximum(qi - 1, 0))

    def past(r, state):
        carry, acc, z, w_prev = state
        n = qi - 1 - r
        z_next = scores(jnp.maximum(n - 1, 0))
        acc = acc + weighted_values(w_prev, n + 1)
        w, carry = weights(z, carry, None)
        return carry, acc, z_next, w

    _, acc, _, w = lax.fori_loop(0, qi, past, (carry, jnp.zeros((2 * blk, LANES), F32), z_next, w))
    acc = acc + weighted_values(w, 0)
    o_ref[...] = jnp.where(lane < HEAD_DIM, acc[:blk], acc[blk:]).astype(o_ref.dtype)


def _sb_prompt(q, kt, vt, li, batch, seq):
    nq = seq // SB_BLOCK
    npair = N_SB_HEADS // 2
    off = N_MOBA_HEADS // 2
    return pl.pallas_call(
        _sb_prompt_kernel,
        grid=(batch, npair, nq),
        in_specs=[
            pl.BlockSpec((SB_BLOCK, LANES), lambda b, hp, qi: (b * nq + qi, off + hp)),
            pl.BlockSpec((None, None, LANES, seq), lambda b, hp, qi: (li, b, off + hp, 0)),
            pl.BlockSpec((None, None, LANES, seq), lambda b, hp, qi: (li, b, off + hp, 0)),
            pl.BlockSpec((2 * SB_BLOCK, SB_BLOCK), lambda b, hp, qi: (0, 0)),
        ],
        out_specs=pl.BlockSpec((SB_BLOCK, LANES), lambda b, hp, qi: (b * nq + qi, hp)),
        out_shape=jax.ShapeDtypeStruct((batch * seq, N_SB_HEADS * HEAD_DIM), BF16),
        compiler_params=_cp("arbitrary", "arbitrary", "arbitrary"),
        name="sb_prompt",
    )(q, kt, vt, _later_matrix(SB_BLOCK))


def _head_columns(g, first):
    return jnp.stack([g[:, first + h:first + h + 1] for h in range(N_REC_HEADS)])


def _mlstm_prompt_kernel(q_ref, k_ref, v_ref, o_ref, gate_ref, hn_ref, hh_ref, c_ref, n_ref, m_ref):
    lc = REC_CHUNK
    dh = LANES
    hds = N_REC_HEADS

    @pl.when(pl.program_id(1) == 0)
    def _():
        c_ref[...] = jnp.zeros_like(c_ref)
        n_ref[...] = jnp.zeros_like(n_ref)
        m_ref[...] = jnp.zeros_like(m_ref)

    ri = _iota((1, lc, lc), 1)
    ci = _iota((1, lc, lc), 2)
    eye = ri == ci
    tril = ci <= ri
    q3 = q_ref[...]
    k3 = k_ref[...] * (dh ** -0.5)
    v3 = v_ref[...]
    g = gate_ref[...]
    i_col = _head_columns(g, 0)
    lf_col = _log_sigmoid(_head_columns(g, hds))
    c_old = c_ref[...]
    n_old = n_ref[...]
    m_old = m_ref[...]
    qb, kb, vb = q3.astype(BF16), k3.astype(BF16), v3.astype(BF16)
    cb = c_old.astype(BF16)

    lf_row = jnp.sum(jnp.where(eye, lf_col, 0.0), axis=1, keepdims=True)
    i_row = jnp.sum(jnp.where(eye, i_col, 0.0), axis=1, keepdims=True)
    b_col = jnp.sum(jnp.where(tril, lf_row, 0.0), axis=2, keepdims=True)
    b_row = jnp.sum(jnp.where(ri <= ci, lf_col, 0.0), axis=1, keepdims=True)
    a_row = i_row - b_row
    a_col = i_col - b_col
    cm_col = jnp.max(jnp.where(tril, a_row, -jnp.inf), axis=2, keepdims=True)
    mt_col = b_col + jnp.maximum(m_old, cm_col)
    inter = jnp.exp(m_old + b_col - mt_col)
    dmat = jnp.where(tril, jnp.exp(jnp.where(tril, a_row + (b_col - mt_col), 0.0)), 0.0)
    s = dmat * jnp.stack([_dot_nt(qb[h], kb[h]) for h in range(hds)])
    sb = s.astype(BF16)
    num = inter * jnp.stack([_dot(qb[h], cb[h]) for h in range(hds)]) \
        + jnp.stack([_dot(sb[h], vb[h]) for h in range(hds)])
    den = inter * jnp.sum(q3 * n_old, axis=2, keepdims=True) + jnp.sum(s, axis=2, keepdims=True)
    hh = num / jnp.maximum(jnp.abs(den), jnp.exp(-mt_col))
    b_last = b_col[:, lc - 1:lc, :]
    m_new = mt_col[:, lc - 1:lc, :]
    w_last = jnp.exp(a_col + (b_last - m_new))
    decay = jnp.exp(m_old + b_last - m_new)
    kw = k3 * w_last
    kwb = kw.astype(BF16)
    c_ref[...] = decay * c_old + jnp.stack([_dot_tn(kwb[h], vb[h]) for h in range(hds)])
    n_ref[...] = decay * n_old + jnp.sum(kw, axis=1, keepdims=True)
    m_ref[...] = m_new

    y = hh * lax.rsqrt(jnp.mean(hh * hh, axis=2, keepdims=True) + EPS) * hn_ref[...]
    out = (y * _sigmoid(o_ref[...])).astype(hh_ref.dtype)
    for h in range(hds):
        hh_ref[:, h * dh:(h + 1) * dh] = out[h]


def _mlstm_prompt(proj, gates, hn_g, batch, seq):
    lc = REC_CHUNK
    nc = seq // lc
    hds, dh = N_REC_HEADS, LANES
    d = hds * dh

    def part(kind):
        return pl.BlockSpec((hds, lc, dh), lambda b, c: (kind, b * nc + c, 0))

    return pl.pallas_call(
        _mlstm_prompt_kernel,
        grid=(batch, nc),
        in_specs=[
            part(0), part(1), part(2), part(3),
            pl.BlockSpec((lc, LANES), lambda b, c: (b * nc + c, 0)),
            pl.BlockSpec((hds, 1, dh), lambda b, c: (0, 0, 0)),
        ],
        out_specs=[
            pl.BlockSpec((lc, d), lambda b, c: (b * nc + c, 0)),
            pl.BlockSpec((None, hds, dh, dh), lambda b, c: (b, 0, 0, 0)),
            pl.BlockSpec((None, hds, 1, dh), lambda b, c: (b, 0, 0, 0)),
            pl.BlockSpec((None, hds, 1, 1), lambda b, c: (b, 0, 0, 0)),
        ],
        out_shape=[
            jax.ShapeDtypeStruct((batch * seq, d), BF16),
            jax.ShapeDtypeStruct((batch, hds, dh, dh), F32),
            jax.ShapeDtypeStruct((batch, hds, 1, dh), F32),
            jax.ShapeDtypeStruct((batch, hds, 1, 1), F32),
        ],
        compiler_params=_cp("arbitrary", "arbitrary"),
        name="mlstm_prompt",
    )(proj, proj, proj, proj, gates, hn_g.reshape(hds, 1, dh))


REC_STEP_BATCH = 8


def _mlstm_step_kernel(p_ref, gate_ref, hn_ref, c0_ref, n0_ref, m0_ref, *rest, n_prev):
    if n_prev:
        cprev_ref, hh_ref, c_ref, n_ref, m_ref = rest
        c_ref[:n_prev] = cprev_ref[...]
    else:
        hh_ref, c_ref, n_ref, m_ref = rest
    dh = LANES
    hds = N_REC_HEADS
    eye = _iota((1, dh, dh), 1) == _iota((1, dh, dh), 2)
    hn = hn_ref[...]
    for bb in range(REC_STEP_BATCH):
        row = slice(bb, bb + 1)
        q3 = p_ref[0:hds, row, :]
        k3 = p_ref[hds:2 * hds, row, :] * (dh ** -0.5)
        v3 = p_ref[2 * hds:3 * hds, row, :]
        o3 = p_ref[3 * hds:4 * hds, row, :]
        g = gate_ref[row, :]
        i_pre = _head_columns(g, 0)
        lf = _log_sigmoid(_head_columns(g, hds))
        m_old = m0_ref[bb]
        n_old = n0_ref[bb]
        c_old = c0_ref[bb]
        mt = lf + jnp.maximum(m_old, i_pre - lf)
        inter = jnp.exp(m_old + lf - mt)
        w_last = jnp.exp(i_pre - mt)
        q_col = jnp.sum(jnp.where(eye, q3, 0.0), axis=2, keepdims=True)
        k_col = jnp.sum(jnp.where(eye, k3, 0.0), axis=2, keepdims=True)
        qc = jnp.sum(q_col * c_old, axis=1, keepdims=True)
        qk = jnp.sum(q3 * k3, axis=2, keepdims=True)
        qn = jnp.sum(q3 * n_old, axis=2, keepdims=True)
        s = w_last * qk
        num = inter * qc + s * v3
        den = inter * qn + s
        hh = num / jnp.maximum(jnp.abs(den), jnp.exp(-mt))
        c_ref[n_prev, bb] = inter * c_old + (w_last * k_col) * v3
        n_ref[bb] = inter * n_old + w_last * k3
        m_ref[bb] = mt
        y = hh * lax.rsqrt(jnp.mean(hh * hh, axis=2, keepdims=True) + EPS) * hn
        out = (y * _sigmoid(o3)).astype(hh_ref.dtype)
        for h in range(hds):
            hh_ref[row, h * dh:(h + 1) * dh] = out[h]


def _mlstm_step(proj, gates, hn_g, state_c, state_n, state_m, j_layer, c_prev):
    nb = proj.shape[1]
    hds, dh = N_REC_HEADS, LANES
    d = hds * dh
    bb = REC_STEP_BATCH
    n_prev = j_layer
    in_specs = [
        pl.BlockSpec((4 * hds, bb, dh), lambda i: (0, i, 0)),
        pl.BlockSpec((bb, LANES), lambda i: (i, 0)),
        pl.BlockSpec((hds, 1, dh), lambda i: (0, 0, 0)),
        pl.BlockSpec((None, bb, hds, dh, dh), lambda i: (j_layer, i, 0, 0, 0)),
        pl.BlockSpec((None, bb, hds, 1, dh), lambda i: (j_layer, i, 0, 0, 0)),
        pl.BlockSpec((None, bb, hds, 1, 1), lambda i: (j_layer, i, 0, 0, 0)),
    ]
    args = [proj, gates, hn_g.reshape(hds, 1, dh), state_c, state_n, state_m]
    if n_prev:
        in_specs.append(pl.BlockSpec((n_prev, bb, hds, dh, dh), lambda i: (0, i, 0, 0, 0)))
        args.append(c_prev)
    return pl.pallas_call(
        functools.partial(_mlstm_step_kernel, n_prev=n_prev),
        grid=(nb // bb,),
        in_specs=in_specs,
        out_specs=[
            pl.BlockSpec((bb, d), lambda i: (i, 0)),
            pl.BlockSpec((n_prev + 1, bb, hds, dh, dh), lambda i: (0, i, 0, 0, 0)),
            pl.BlockSpec((bb, hds, 1, dh), lambda i: (i, 0, 0, 0)),
            pl.BlockSpec((bb, hds, 1, 1), lambda i: (i, 0, 0, 0)),
        ],
        out_shape=[
            jax.ShapeDtypeStruct((nb, d), BF16),
            jax.ShapeDtypeStruct((n_prev + 1, nb, hds, dh, dh), F32),
            jax.ShapeDtypeStruct((nb, hds, 1, dh), F32),
            jax.ShapeDtypeStruct((nb, hds, 1, 1), F32),
        ],
        compiler_params=_cp("arbitrary"),
        name="mlstm_step",
    )(*args)


PAGES_PER_STEP = 8


def _spread_heads(x):
    h, l = x.shape
    return jnp.broadcast_to(x[:, None, :], (h, HEAD_DIM, l)).reshape(h * HEAD_DIM, l)


def _row_to_column(r):
    eye = _iota((LANES, LANES), 0) == _iota((LANES, LANES), 1)
    return jnp.sum(jnp.where(eye, r, 0.0), axis=1, keepdims=True)


def _column_to_row(c):
    eye = _iota((LANES, LANES), 0) == _iota((LANES, LANES), 1)
    return jnp.sum(jnp.where(eye, c, 0.0), axis=0, keepdims=True)


def _decode_attn_kernel(pt_ref, q_ref, kn_ref, vn_ref, slope_ref, u_ref, *refs, page, n_pages, nblk):
    pp = PAGES_PER_STEP
    k_refs, v_refs = refs[:pp], refs[pp:2 * pp]
    o_ref = refs[2 * pp]
    q_bc, sb_acc, sb_carry, mb_m, mb_l, mb_acc, mb_gate = refs[2 * pp + 1:]
    g = pl.program_id(1)
    nsteps = n_pages // pp
    hm = MOBA_COLS
    hd = N_ATT_HEADS * HEAD_DIM
    scale = HEAD_DIM ** -0.5
    pos = n_pages * page
    pages_per_blk = MOBA_BLOCK // page
    blks_per_step = pp // pages_per_blk
    slope = slope_ref[...]
    lane_pos = _iota((1, page), 1)

    @pl.when(g == 0)
    def _():
        for c in range(hd // LANES):
            sl = slice(c * LANES, (c + 1) * LANES)
            q_bc[sl, :] = jnp.broadcast_to(_row_to_column(q_ref[:, sl]), (LANES, page))
        sb_acc[...] = jnp.zeros_like(sb_acc)
        sb_carry[...] = jnp.zeros_like(sb_carry)

    q_all = q_bc[...]
    raws, zs = [], []
    for i in range(pp):
        sc = jnp.sum((k_refs[i][...] * q_all).reshape(N_ATT_HEADS, HEAD_DIM, page), axis=1)
        raws.append(sc[:N_MOBA_HEADS])
        zs.append(sc[N_MOBA_HEADS:] * scale)

    tails = [_softplus_tail(z) for z in zs]
    lks = [-jnp.maximum(z, 0.0) - t for z, t in zip(zs, tails)]
    later_all = _dot(_hilo(jnp.concatenate(lks, axis=0)), u_ref[...])
    carry = sb_carry[:, 0:1]
    acc_sb = sb_acc[...]
    for i in range(pp):
        later = later_all[i * N_SB_HEADS:(i + 1) * N_SB_HEADS] + carry
        w = jnp.exp(jnp.minimum(zs[i], 0.0) - tails[i] + later)
        acc_sb = acc_sb + v_refs[i][hm:, :] * _spread_heads(w)
        carry = carry + jnp.sum(lks[i], axis=-1, keepdims=True)
    sb_acc[...] = acc_sb
    sb_carry[...] = jnp.broadcast_to(carry, (N_SB_HEADS, LANES))

    for half in range(blks_per_step):
        idx = [half * pages_per_blk + t for t in range(pages_per_blk)]
        s_parts = []
        for i in idx:
            pg = n_pages - 1 - (g * pp + i)
            dist = (pos - (pg * page + lane_pos)).astype(F32)
            s_parts.append(raws[i] * scale - slope * dist)
        gate_blk = functools.reduce(lambda a, b: a + b, [jnp.sum(raws[i], axis=-1, keepdims=True) for i in idx])
        m_blk = functools.reduce(jnp.maximum, [jnp.max(s, axis=-1, keepdims=True) for s in s_parts])
        l_blk = jnp.zeros((N_MOBA_HEADS, 1), F32)
        a_blk = jnp.zeros((hm, page), F32)
        for i, s in zip(idx, s_parts):
            p = jnp.exp(s - m_blk)
            l_blk = l_blk + jnp.sum(p, axis=-1, keepdims=True)
            a_blk = a_blk + v_refs[i][:hm, :] * _spread_heads(p)
        n_blk = nblk - 1 - (g * blks_per_step + half)
        mb_m[n_blk] = jnp.broadcast_to(m_blk, (N_MOBA_HEADS, LANES))
        mb_l[n_blk] = jnp.broadcast_to(l_blk, (N_MOBA_HEADS, LANES))
        mb_acc[n_blk] = a_blk
        mb_gate[n_blk] = jnp.broadcast_to(gate_blk * (1.0 / MOBA_BLOCK), (N_MOBA_HEADS, LANES))

    @pl.when(g == nsteps - 1)
    def _():
        gates = [mb_gate[n][:, 0:1] for n in range(nblk)]
        diag = _iota((N_MOBA_HEADS, hm), 0) == _iota((N_MOBA_HEADS, hm), 1) // HEAD_DIM
        q_mb = jnp.where(diag, q_ref[:, :hm], 0.0)
        s_self = jnp.sum(q_mb * kn_ref[:, :hm], axis=-1, keepdims=True) * scale
        sels, m_fin = [], s_self
        for n in range(nblk):
            rank = jnp.zeros((N_MOBA_HEADS, 1), F32)
            for mth in range(nblk):
                if mth == n:
                    continue
                beats = gates[mth] > gates[n]
                if mth < n:
                    beats = beats | (gates[mth] == gates[n])
                rank = rank + beats.astype(F32)
            sel = rank < MOBA_TOPK
            sels.append(sel)
            m_fin = jnp.maximum(m_fin, jnp.where(sel, mb_m[n][:, 0:1], NEG))
        p_self = jnp.exp(s_self - m_fin)
        l_fin = p_self
        a_fin = jnp.zeros((hm, page), F32)
        for n in range(nblk):
            wgt = jnp.where(sels[n], jnp.exp(mb_m[n][:, 0:1] - m_fin), 0.0)
            l_fin = l_fin + wgt * mb_l[n][:, 0:1]
            a_fin = a_fin + _spread_heads(wgt) * mb_acc[n]
        col_mb = jnp.sum(a_fin, axis=-1, keepdims=True)
        col_sb = jnp.sum(acc_sb, axis=-1, keepdims=True)
        w_self = _spread_heads(p_self)
        inv_l = _spread_heads(1.0 / l_fin)
        for c in range(hm // LANES):
            sl = slice(c * LANES, (c + 1) * LANES)
            v_col = _row_to_column(vn_ref[:, sl])
            o_ref[:, sl] = _column_to_row((col_mb[sl] + w_self[sl] * v_col) * inv_l[sl]).astype(o_ref.dtype)
            o_ref[:, hm + c * LANES:hm + (c + 1) * LANES] = _column_to_row(col_sb[sl]).astype(o_ref.dtype)


def _decode_attn(q, k_new, v_new, cache_kt, cache_vt, page_table, j_layer):
    nb = q.shape[0]
    n_pages = page_table.shape[1]
    hd, page = cache_kt.shape[2], cache_kt.shape[3]
    hm = MOBA_COLS
    pp = PAGES_PER_STEP
    nblk = n_pages * page // MOBA_BLOCK

    def page_spec(i):
        return pl.BlockSpec((None, None, hd, page),
                            lambda b, g, pt: (j_layer, pt[b * n_pages + (n_pages - 1 - (g * pp + i))], 0, 0))

    row = pl.BlockSpec((None, 1, hd), lambda b, g, pt: (b, 0, 0))
    grid_spec = pltpu.PrefetchScalarGridSpec(
        num_scalar_prefetch=1,
        grid=(nb, n_pages // pp),
        in_specs=[row, row, row,
                  pl.BlockSpec((N_MOBA_HEADS, 1), lambda b, g, pt: (0, 0)),
                  pl.BlockSpec((2 * page, page), lambda b, g, pt: (0, 0))]
                 + [page_spec(i) for i in range(pp)] + [page_spec(i) for i in range(pp)],
        out_specs=pl.BlockSpec((None, 1, hd), lambda b, g, pt: (b, 0, 0)),
        scratch_shapes=[
            pltpu.VMEM((hd, page), F32),
            pltpu.VMEM((hd - hm, page), F32),
            pltpu.VMEM((N_SB_HEADS, LANES), F32),
            pltpu.VMEM((nblk, N_MOBA_HEADS, LANES), F32),
            pltpu.VMEM((nblk, N_MOBA_HEADS, LANES), F32),
            pltpu.VMEM((nblk, hm, page), F32),
            pltpu.VMEM((nblk, N_MOBA_HEADS, LANES), F32),
        ],
    )
    out = pl.pallas_call(
        functools.partial(_decode_attn_kernel, page=page, n_pages=n_pages, nblk=nblk),
        grid_spec=grid_spec,
        out_shape=jax.ShapeDtypeStruct((nb, 1, hd), BF16),
        compiler_params=_cp("arbitrary", "arbitrary"),
        name="decode_attn",
    )(page_table.reshape(-1), q.reshape(nb, 1, hd), k_new.reshape(nb, 1, hd), v_new.reshape(nb, 1, hd),
      jnp.asarray(_alibi_slopes(), F32).reshape(N_MOBA_HEADS, 1), _later_matrix(page),
      *([cache_kt] * pp), *([cache_vt] * pp))
    return out.reshape(nb, hd)


ROW_TILE = 512
QKV_ROW_TILE = 256


def _decoder(x, mod_arr, rows_per_batch, w, cache=None, state=None):
    m, d = x.shape
    depth = w["mlp_w1"].shape[0]
    batch = m // rows_per_batch
    tm = m if rows_per_batch == 1 else min(ROW_TILE, rows_per_batch)
    kv, c_stack, cs, ns, ms = None, None, [], [], []
    for l in range(depth):
        mod = _Mod(mod_arr, l, d, tm, rows_per_batch)
        j = l // 2
        if l % 2 == 0:
            if cache is None:
                mod_qkv = _Mod(mod_arr, l, d, min(QKV_ROW_TILE, rows_per_batch), rows_per_batch)
                q, kt, vt = _qkv_proj(x, w["norm1_g"], mod_qkv, w, j, rows_per_batch, False, kv)
                oa = _moba_prompt(q, kt, vt, j, batch, rows_per_batch)
                ob = _sb_prompt(q, kt, vt, j, batch, rows_per_batch)
                o = jnp.concatenate([oa, ob], axis=1)
            else:
                q, kt, vt, k, v = _qkv_proj(x, w["norm1_g"], mod, w, j, rows_per_batch, True, kv)
                o = _decode_attn(q, k, v, cache[0], cache[1], cache[2], j)
            kv = (kt, vt)
            x = _res_matmul(o, w["att_w_out"], j, x, mod, 2)
        else:
            proj, gates = _rec_in_proj(x, w["norm1_g"], mod, w["rec_w_main"], w["rec_w_gate"], w["rec_b_gate"], j)
            if state is None:
                hh, c_new, n_new, m_new = _mlstm_prompt(proj, gates, w["rec_hn_g"][j], batch, rows_per_batch)
                cs.append(c_new)
            else:
                hh, c_stack, n_new, m_new = _mlstm_step(proj, gates, w["rec_hn_g"][j], state[0], state[1], state[2],
                                                        j, c_stack)
            ns.append(n_new.reshape(batch, N_REC_HEADS, LANES))
            ms.append(m_new.reshape(batch, N_REC_HEADS))
            x = _res_matmul(hh, w["rec_w_out"], j, x, mod, 2)
        hid = _nm_matmul(x, w["norm2_g"], mod, (4, 3), w["mlp_w1"], l, relu2=True, out_dtype=BF16)
        x = _res_matmul(hid, w["mlp_w2"], l, x, mod, 5)
    c_all = jnp.stack(cs) if state is None else c_stack
    return x, kv[0], kv[1], c_all, jnp.stack(ns), jnp.stack(ms)


def _rows_from_feature_major(t):
    na, b, _, s = t.shape
    return jnp.transpose(t.reshape(na, b, N_ATT_HEADS, HEAD_DIM, s), (0, 1, 4, 2, 3))


def kernel(x_prompt, x_sample, cache_k, cache_v, page_table, state_C, state_n, state_m, c_prompt, c_sample,
           ada_w, ada_b, norm1_g, norm2_g, att_w_in, att_qn_g, att_kn_g, att_w_out,
           rec_w_in, rec_b_i, rec_b_f, rec_hn_g, rec_w_out, mlp_w1, mlp_w2):
    bp, seq, d = x_prompt.shape
    bs = x_sample.shape[0]
    depth = ada_w.shape[0]
    hds = N_REC_HEADS
    n_rec = rec_w_in.shape[0]
    hd = N_ATT_HEADS * HEAD_DIM

    mod = _adaln_mod(jnp.concatenate([c_prompt, c_sample], axis=0), ada_w, ada_b)
    mod_p = mod[:, :bp].reshape(depth, bp, 1, 6 * d)
    mod_s = mod[:, bp:]

    gate_w = jnp.pad(rec_w_in[:, :, 4 * d:], ((0, 0), (0, 0), (0, LANES - 2 * hds)))
    gate_b = jnp.pad(jnp.concatenate([rec_b_i, rec_b_f], axis=1), ((0, 0), (0, LANES - 2 * hds)))
    w_in16 = att_w_in.astype(BF16)
    w = {
        "norm1_g": norm1_g.reshape(depth, 1, d),
        "norm2_g": norm2_g.reshape(depth, 1, d),
        "att_wq": w_in16[:, :, :hd],
        "att_wk": w_in16[:, :, hd:2 * hd],
        "att_wv": w_in16[:, :, 2 * hd:],
        "att_wkt": jnp.transpose(w_in16[:, :, hd:2 * hd], (0, 2, 1)),
        "att_wvt": jnp.transpose(w_in16[:, :, 2 * hd:], (0, 2, 1)),
        "att_qn_g": jnp.tile(att_qn_g, (1, LANES // HEAD_DIM)).reshape(-1, 1, LANES),
        "att_kn_g": jnp.tile(att_kn_g, (1, LANES // HEAD_DIM)).reshape(-1, 1, LANES),
        "att_kn_gc": att_kn_g.reshape(-1, HEAD_DIM, 1),
        "att_w_out": att_w_out.astype(BF16),
        "rec_w_main": rec_w_in[:, :, :4 * d].astype(BF16),
        "rec_w_gate": gate_w.astype(BF16),
        "rec_b_gate": gate_b.reshape(n_rec, 1, LANES),
        "rec_hn_g": rec_hn_g,
        "rec_w_out": rec_w_out.astype(BF16),
        "mlp_w1": mlp_w1.astype(BF16),
        "mlp_w2": mlp_w2.astype(BF16),
    }

    yp, ktp, vtp, cp_, np_, mp = _decoder(x_prompt.reshape(bp * seq, d), mod_p, seq, w)

    def feature_major_cache(c):
        nl, pool, page = c.shape[:3]
        return jnp.transpose(c, (0, 1, 3, 4, 2)).reshape(nl, pool, hd, page)

    ys, kts, vts, csm, nsm, msm = _decoder(x_sample.reshape(bs, d), mod_s, 1, w,
                                           cache=(feature_major_cache(cache_k), feature_major_cache(cache_v),
                                                  page_table),
                                           state=(state_C, state_n.reshape(n_rec, bs, hds, 1, LANES),
                                                  state_m.reshape(n_rec, bs, hds, 1, 1)))
    k_s = jnp.transpose(_rows_from_feature_major(kts), (0, 2, 1, 3, 4))
    v_s = jnp.transpose(_rows_from_feature_major(vts), (0, 2, 1, 3, 4))
    return (yp.reshape(bp, seq, d), ys.reshape(bs, 1, d),
            _rows_from_feature_major(ktp), _rows_from_feature_major(vtp), k_s, v_s,
            cp_, np_, mp, csm, nsm, msm)
```

```python
import functools

import jax
import jax.numpy as jnp
from jax import lax
from jax.experimental import pallas as pl
from jax.experimental.pallas import tpu as pltpu

F32 = jnp.float32
BF16 = jnp.bfloat16

EPS = 1e-6
HEAD_DIM = 64
N_MOBA_HEADS = 8
N_SB_HEADS = 8
N_ATT_HEADS = N_MOBA_HEADS + N_SB_HEADS
MOBA_BLOCK = 256
MOBA_TOPK = 3
N_REC_HEADS = 8
REC_CHUNK = 64
LANES = 128
NEG = -1e30
VMEM_LIMIT = 56 * 1024 * 1024
MOBA_COLS = N_MOBA_HEADS * HEAD_DIM


def _cp(*sem):
    return pltpu.CompilerParams(dimension_semantics=sem, vmem_limit_bytes=VMEM_LIMIT)


def _sigmoid(x):
    return 1.0 / (1.0 + jnp.exp(-x))


def _softplus_tail(z):
    return jnp.log(1.0 + jnp.exp(-jnp.abs(z)))


def _log_sigmoid(z):
    return jnp.minimum(z, 0.0) - _softplus_tail(z)


def _dot(a, b):
    return jnp.dot(a, b, preferred_element_type=F32)


def _dot_nt(a, b):
    return lax.dot_general(a, b, (((1,), (1,)), ((), ())), preferred_element_type=F32)


def _dot_tn(a, b):
    return lax.dot_general(a, b, (((0,), (0,)), ((), ())), preferred_element_type=F32)


def _hilo(x):
    hi = x.astype(BF16)
    lo = (x - hi.astype(F32)).astype(BF16)
    return jnp.concatenate([hi, lo], axis=-1)


def _iota(shape, dim):
    return lax.broadcasted_iota(jnp.int32, shape, dim)


def _alibi_slopes():
    return [2.0 ** (-8.0 * (h + 1) / N_MOBA_HEADS) for h in range(N_MOBA_HEADS)]


def _mod_kernel(c_ref, w_ref, b_ref, o_ref):
    c = c_ref[...]
    s = c * _sigmoid(c)
    o_ref[...] = _dot(s.astype(BF16), w_ref[...].astype(BF16)) + b_ref[...]


def _adaln_mod(c_all, ada_w, ada_b):
    depth, d, n = ada_w.shape
    m = c_all.shape[0]
    tn = 1024
    return pl.pallas_call(
        _mod_kernel,
        grid=(depth, n // tn),
        in_specs=[
            pl.BlockSpec((m, d), lambda l, j: (0, 0)),
            pl.BlockSpec((None, d, tn), lambda l, j: (l, 0, j)),
            pl.BlockSpec((None, 1, tn), lambda l, j: (l, 0, j)),
        ],
        out_specs=pl.BlockSpec((None, m, tn), lambda l, j: (l, 0, j)),
        out_shape=jax.ShapeDtypeStruct((depth, m, n), F32),
        compiler_params=_cp("arbitrary", "arbitrary"),
        name="adaln_mod",
    )(c_all, ada_w, ada_b.reshape(depth, 1, n))


class _Mod:
    def __init__(self, arr, l, d, tm, rows_per_batch):
        self.arr, self.l, self.d, self.tm = arr, l, d, tm
        self.per_row = rows_per_batch == 1
        self.blocks_per_batch = None if self.per_row else rows_per_batch // tm

    def spec(self, chunk):
        d, l = self.d, self.l
        if self.per_row:
            return pl.BlockSpec((None, self.tm, d), lambda i: (l, i, chunk))
        bpb = self.blocks_per_batch
        return pl.BlockSpec((None, None, 1, d), lambda i: (l, i // bpb, 0, chunk))


def _norm_mod(x_ref, g_ref, sc_ref, sh_ref):
    x = x_ref[...]
    y = x * lax.rsqrt(jnp.mean(x * x, axis=-1, keepdims=True) + EPS) * g_ref[...]
    return (y * (1.0 + sc_ref[...]) + sh_ref[...]).astype(BF16)


COL_CHUNK = 1024


def _nm_mm_kernel(x_ref, g_ref, sc_ref, sh_ref, w_ref, o_ref, *, relu2):
    h = _norm_mod(x_ref, g_ref, sc_ref, sh_ref)
    for c in range(w_ref.shape[1] // COL_CHUNK):
        sl = slice(c * COL_CHUNK, (c + 1) * COL_CHUNK)
        acc = _dot(h, w_ref[:, sl])
        if relu2:
            acc = jnp.square(jnp.maximum(acc, 0.0))
        o_ref[:, sl] = acc.astype(o_ref.dtype)


def _nm_matmul(x, g_all, mod, chunks, w_all, l_w, *, relu2, out_dtype):
    m, d = x.shape
    n = w_all.shape[-1]
    tm, l = mod.tm, mod.l
    return pl.pallas_call(
        functools.partial(_nm_mm_kernel, relu2=relu2),
        grid=(m // tm,),
        in_specs=[
            pl.BlockSpec((tm, d), lambda i: (i, 0)),
            pl.BlockSpec((None, 1, d), lambda i: (l, 0, 0)),
            mod.spec(chunks[0]),
            mod.spec(chunks[1]),
            pl.BlockSpec((None, d, n), lambda i: (l_w, 0, 0)),
        ],
        out_specs=pl.BlockSpec((tm, n), lambda i: (i, 0)),
        out_shape=jax.ShapeDtypeStruct((m, n), out_dtype),
        compiler_params=_cp("arbitrary"),
        name="nm_matmul",
    )(x, g_all, mod.arr, mod.arr, w_all)


def _rec_in_kernel(x_ref, g_ref, sc_ref, sh_ref, w_ref, wg_ref, bg_ref, o_ref, gate_ref):
    h = _norm_mod(x_ref, g_ref, sc_ref, sh_ref)
    gate_ref[...] = _dot(h, wg_ref[...]) + bg_ref[...]
    dm = N_REC_HEADS * LANES
    for kind in range(w_ref.shape[1] // dm):
        acc = _dot(h, w_ref[:, kind * dm:(kind + 1) * dm])
        for hh in range(N_REC_HEADS):
            o_ref[kind * N_REC_HEADS + hh] = acc[:, hh * LANES:(hh + 1) * LANES]


def _rec_in_proj(x, g_all, mod, w_all, wg_all, bg_all, j_layer):
    m, d = x.shape
    n = w_all.shape[-1]
    tm, l = mod.tm, mod.l
    return pl.pallas_call(
        _rec_in_kernel,
        grid=(m // tm,),
        in_specs=[
            pl.BlockSpec((tm, d), lambda i: (i, 0)),
            pl.BlockSpec((None, 1, d), lambda i: (l, 0, 0)),
            mod.spec(1),
            mod.spec(0),
            pl.BlockSpec((None, d, n), lambda i: (j_layer, 0, 0)),
            pl.BlockSpec((None, d, LANES), lambda i: (j_layer, 0, 0)),
            pl.BlockSpec((None, 1, LANES), lambda i: (j_layer, 0, 0)),
        ],
        out_specs=[
            pl.BlockSpec((n // LANES, tm, LANES), lambda i: (0, i, 0)),
            pl.BlockSpec((tm, LANES), lambda i: (i, 0)),
        ],
        out_shape=[jax.ShapeDtypeStruct((n // LANES, m, LANES), F32), jax.ShapeDtypeStruct((m, LANES), F32)],
        compiler_params=_cp("arbitrary"),
        name="rec_in_proj",
    )(x, g_all, mod.arr, mod.arr, w_all, wg_all, bg_all)


def _head_rmsnorm_rows(a, gain, gmat):
    ms = _dot(_hilo(a * a), gmat)
    return a * lax.rsqrt(ms + EPS) * gain


def _write_normed_rows(o_ref, a, gain_ref, gm_ref):
    for c in range(MOBA_COLS // LANES):
        sl = slice(c * LANES, (c + 1) * LANES)
        o_ref[:, sl] = _head_rmsnorm_rows(a[:, sl], gain_ref[...], gm_ref[...])
    o_ref[:, MOBA_COLS:] = a[:, MOBA_COLS:]


def _qkv_kernel(x_ref, g_ref, sc_ref, sh_ref, wq_ref, wkt_ref, wvt_ref, qg_ref, kgc_ref, gm_ref, *rest, rows, n_prev):
    rest = list(rest)
    if n_prev:
        ktp_ref, vtp_ref = rest[:2]
        rest = rest[2:]
    if rows:
        wk_ref, wv_ref, kg_ref, q_ref, kt_ref, vt_ref, k_ref, v_ref = rest
    else:
        q_ref, kt_ref, vt_ref = rest
    if n_prev:
        kt_ref[:n_prev] = ktp_ref[...]
        vt_ref[:n_prev] = vtp_ref[...]
    h = _norm_mod(x_ref, g_ref, sc_ref, sh_ref)
    tm = h.shape[0]
    _write_normed_rows(q_ref, _dot(h, wq_ref[...]), qg_ref, gm_ref)
    kt = _dot_nt(wkt_ref[...], h)
    a = kt[:MOBA_COLS].reshape(N_MOBA_HEADS, HEAD_DIM, tm)
    ms = jnp.mean(a * a, axis=1, keepdims=True)
    kt_ref[n_prev, :MOBA_COLS, :] = (a * lax.rsqrt(ms + EPS)
                                     * kgc_ref[...].reshape(1, HEAD_DIM, 1)).reshape(MOBA_COLS, tm)
    kt_ref[n_prev, MOBA_COLS:, :] = kt[MOBA_COLS:]
    vt_ref[n_prev] = _dot_nt(wvt_ref[...], h)
    if rows:
        _write_normed_rows(k_ref, _dot(h, wk_ref[...]), kg_ref, gm_ref)
        v_ref[...] = _dot(h, wv_ref[...])


def _group_mean_matrix():
    r = jnp.arange(2 * LANES) % LANES
    c = jnp.arange(LANES)
    return jnp.where((r[:, None] // HEAD_DIM) == (c[None, :] // HEAD_DIM), 1.0 / HEAD_DIM, 0.0).astype(BF16)


def _qkv_proj(x, g_all, mod, w, j_layer, rows_per_batch, rows, prev):
    m, d = x.shape
    hd = N_ATT_HEADS * HEAD_DIM
    tm, l = mod.tm, mod.l
    seq = m if rows_per_batch == 1 else rows_per_batch
    batch = m // seq
    bps = seq // tm
    wspec = pl.BlockSpec((None, d, hd), lambda i: (j_layer, 0, 0))
    wtspec = pl.BlockSpec((None, hd, d), lambda i: (j_layer, 0, 0))
    gspec = pl.BlockSpec((None, 1, LANES), lambda i: (j_layer, 0, 0))
    in_specs = [
        pl.BlockSpec((tm, d), lambda i: (i, 0)),
        pl.BlockSpec((None, 1, d), lambda i: (l, 0, 0)),
        mod.spec(1),
        mod.spec(0),
        wspec, wtspec, wtspec, gspec,
        pl.BlockSpec((None, HEAD_DIM, 1), lambda i: (j_layer, 0, 0)),
        pl.BlockSpec((2 * LANES, LANES), lambda i: (0, 0)),
    ]
    args = [x, g_all, mod.arr, mod.arr, w["att_wq"], w["att_wkt"], w["att_wvt"], w["att_qn_g"], w["att_kn_gc"],
            _group_mean_matrix()]
    n_prev = j_layer
    row_out = jax.ShapeDtypeStruct((m, hd), F32)
    t_out = jax.ShapeDtypeStruct((n_prev + 1, batch, hd, seq), F32)
    row_spec = pl.BlockSpec((tm, hd), lambda i: (i, 0))
    t_spec = pl.BlockSpec((n_prev + 1, None, hd, tm), lambda i: (0, i // bps, 0, i % bps))
    out_shape, out_specs = [row_out, t_out, t_out], [row_spec, t_spec, t_spec]
    if n_prev:
        prev_spec = pl.BlockSpec((n_prev, None, hd, tm), lambda i: (0, i // bps, 0, i % bps))
        in_specs += [prev_spec, prev_spec]
        args += list(prev)
    if rows:
        in_specs += [wspec, wspec, gspec]
        args += [w["att_wk"], w["att_wv"], w["att_kn_g"]]
        out_shape += [row_out, row_out]
        out_specs += [row_spec, row_spec]
    return pl.pallas_call(
        functools.partial(_qkv_kernel, rows=rows, n_prev=n_prev),
        grid=(m // tm,),
        in_specs=in_specs,
        out_specs=out_specs,
        out_shape=out_shape,
        compiler_params=_cp("arbitrary"),
        name="qkv_proj",
    )(*args)


def _res_mm_kernel(a_ref, w_ref, r_ref, gate_ref, o_ref):
    o_ref[...] = r_ref[...] + gate_ref[...] * _dot(a_ref[...], w_ref[...])


def _res_matmul(a, w_all, l_w, res, mod, gate_chunk):
    m, kdim = a.shape
    n = w_all.shape[-1]
    tm = mod.tm
    return pl.pallas_call(
        _res_mm_kernel,
        grid=(m // tm,),
        in_specs=[
            pl.BlockSpec((tm, kdim), lambda i: (i, 0)),
            pl.BlockSpec((None, kdim, n), lambda i: (l_w, 0, 0)),
            pl.BlockSpec((tm, n), lambda i: (i, 0)),
            mod.spec(gate_chunk),
        ],
        out_specs=pl.BlockSpec((tm, n), lambda i: (i, 0)),
        out_shape=jax.ShapeDtypeStruct((m, n), F32),
        compiler_params=_cp("arbitrary"),
        name="res_matmul",
    )(a, w_all, res, mod.arr)


def _moba_prompt_kernel(q_ref, kt_ref, vt_ref, slope_ref, o_ref, kmean_ref, *, nb):
    qi = pl.program_id(2)
    blk = MOBA_BLOCK
    scale = HEAD_DIM ** -0.5
    lane = _iota((1, LANES), 1)
    row = _iota((blk, blk), 0)
    col = _iota((blk, blk), 1)
    blk_id = _iota((blk, LANES), 1)
    blk_lane = _iota((LANES, LANES), 1)
    q = q_ref[...]

    @pl.when(qi == 0)
    def _():
        kmean_t = jnp.zeros((LANES, LANES), F32)
        for n in range(nb):
            mean_n = jnp.mean(kt_ref[:, n * blk:(n + 1) * blk], axis=-1, keepdims=True)
            kmean_t = jnp.where(blk_lane == n, mean_n, kmean_t)
        kmean_ref[...] = kmean_t

    kmean_t = kmean_ref[...]
    heads = []
    for h2 in range(2):
        hmask = (lane < HEAD_DIM) if h2 == 0 else (lane >= HEAD_DIM)
        slope = slope_ref[:, h2 * HEAD_DIM:h2 * HEAD_DIM + 1]
        qf = jnp.where(hmask, q, 0.0)
        gate = jnp.dot(qf, kmean_t, precision=lax.Precision.HIGHEST, preferred_element_type=F32)
        heads.append((slope, slope * (row - col).astype(F32), (qf * scale).astype(BF16),
                      jnp.where(blk_id < qi, gate, -jnp.inf)))

    def block(n, carries, own):
        start = pl.multiple_of(n * blk, blk)
        ktn = kt_ref[:, pl.ds(start, blk)].astype(BF16)
        vtn = vt_ref[:, pl.ds(start, blk)].astype(BF16)
        blk_dist = ((qi - n) * blk).astype(F32)
        new = []
        for (slope, in_blk, qh, gate), (m, l, acc) in zip(heads, carries):
            s = _dot(qh, ktn) - in_blk - slope * blk_dist
            if own:
                keep = col <= row
            else:
                g_n = jnp.sum(jnp.where(blk_id == n, gate, 0.0), axis=-1, keepdims=True)
                beats = (gate > g_n) | ((gate == g_n) & (blk_id < n))
                keep = jnp.sum(beats.astype(F32), axis=-1, keepdims=True) < MOBA_TOPK
            s = jnp.where(keep, s, NEG)
            m_new = jnp.maximum(m, jnp.max(s, axis=-1, keepdims=True))
            alpha = jnp.exp(m - m_new)
            p = jnp.exp(s - m_new)
            new.append((m_new, alpha * l + jnp.sum(p, axis=-1, keepdims=True),
                        alpha * acc + _dot_nt(p.astype(BF16), vtn)))
        return tuple(new)

    init = (jnp.full((blk, 1), NEG, F32), jnp.zeros((blk, 1), F32), jnp.zeros((blk, LANES), F32))
    carries = lax.fori_loop(0, qi, lambda n, c: block(n, c, False), (init, init))
    (_, l0, acc0), (_, l1, acc1) = block(qi, carries, True)
    o_ref[...] = jnp.where(lane < HEAD_DIM, acc0 / l0, acc1 / l1).astype(o_ref.dtype)


def _alibi_table():
    t = jnp.asarray(_alibi_slopes(), F32).reshape(N_MOBA_HEADS // 2, 2, 1)
    return jnp.broadcast_to(t, (N_MOBA_HEADS // 2, 2, HEAD_DIM)).reshape(N_MOBA_HEADS // 2, 1, LANES)


SB_BLOCK = 256


def _later_matrix(n):
    j = jnp.arange(2 * n) % n
    s = jnp.arange(n)
    return (j[:, None] > s[None, :]).astype(BF16)


def _sb_prompt_kernel(q_ref, kt_ref, vt_ref, u_ref, o_ref):
    qi = pl.program_id(2)
    blk = SB_BLOCK
    scale = HEAD_DIM ** -0.5
    lane = _iota((1, LANES), 1)
    row = _iota((2 * blk, blk), 0) % blk
    col = _iota((2 * blk, blk), 1)
    q = q_ref[...] * scale
    qh = jnp.concatenate([jnp.where(lane < HEAD_DIM, q, 0.0), jnp.where(lane >= HEAD_DIM, q, 0.0)],
                         axis=0).astype(BF16)

    def scores(n):
        return _dot(qh, kt_ref[:, pl.ds(pl.multiple_of(n * blk, blk), blk)].astype(BF16))

    def weights(z, carry, mask):
        sp = jnp.maximum(z, 0.0) + _softplus_tail(z)
        ls = z - sp
        if mask is not None:
            sp = jnp.where(mask, sp, 0.0)
        later = _dot(_hilo(sp), u_ref[...]) + carry
        w = jnp.exp(ls - later)
        if mask is not None:
            w = jnp.where(mask, w, 0.0)
        return w.astype(BF16), carry + jnp.sum(sp, axis=-1, keepdims=True)

    def weighted_values(w, n):
        return _dot_nt(w, vt_ref[:, pl.ds(pl.multiple_of(n * blk, blk), blk)].astype(BF16))

    w, carry = weights(scores(qi), jnp.zeros((2 * blk, 1), F32), col < row)
    z_next = scores(jnp.maximum(qi - 1, 0))

    def past(r, state):
        carry, acc, z, w_prev = state
        n = qi - 1 - r
        z_next = scores(jnp.maximum(n - 1, 0))
        acc = acc + weighted_values(w_prev, n + 1)
        w, carry = weights(z, carry, None)
        return carry, acc, z_next, w

    _, acc, _, w = lax.fori_loop(0, qi, past, (carry, jnp.zeros((2 * blk, LANES), F32), z_next, w))
    acc = acc + weighted_values(w, 0)
    o_ref[...] = jnp.where(lane < HEAD_DIM, acc[:blk], acc[blk:]).astype(o_ref.dtype)


def _prompt_attn_kernel(q_ref, kt_ref, vt_ref, slope_ref, u_ref, o_ref, kmean_ref, *, nb):
    hp = pl.program_id(1)

    @pl.when(hp < N_MOBA_HEADS // 2)
    def _():
        _moba_prompt_kernel(q_ref, kt_ref, vt_ref, slope_ref, o_ref, kmean_ref, nb=nb)

    @pl.when(hp >= N_MOBA_HEADS // 2)
    def _():
        _sb_prompt_kernel(q_ref, kt_ref, vt_ref, u_ref, o_ref)


def _prompt_attn(q, kt, vt, li, batch, seq):
    assert MOBA_BLOCK == SB_BLOCK
    nb = seq // MOBA_BLOCK
    n_moba = N_MOBA_HEADS // 2
    hd = N_ATT_HEADS * HEAD_DIM
    return pl.pallas_call(
        functools.partial(_prompt_attn_kernel, nb=nb),
        grid=(batch, N_ATT_HEADS // 2, nb),
        in_specs=[
            pl.BlockSpec((MOBA_BLOCK, LANES), lambda b, hp, qi: (b * nb + qi, hp)),
            pl.BlockSpec((None, None, LANES, seq), lambda b, hp, qi: (li, b, hp, 0)),
            pl.BlockSpec((None, None, LANES, seq), lambda b, hp, qi: (li, b, hp, 0)),
            pl.BlockSpec((None, 1, LANES), lambda b, hp, qi: (jnp.minimum(hp, n_moba - 1), 0, 0)),
            pl.BlockSpec((2 * SB_BLOCK, SB_BLOCK), lambda b, hp, qi: (0, 0)),
        ],
        out_specs=pl.BlockSpec((MOBA_BLOCK, LANES), lambda b, hp, qi: (b * nb + qi, hp)),
        out_shape=jax.ShapeDtypeStruct((batch * seq, hd), BF16),
        scratch_shapes=[pltpu.VMEM((LANES, LANES), F32)],
        compiler_params=_cp("arbitrary", "arbitrary", "arbitrary"),
        name="prompt_attn",
    )(q, kt, vt, _alibi_table(), _later_matrix(SB_BLOCK))


def _head_columns(g, first):
    return jnp.stack([g[:, first + h:first + h + 1] for h in range(N_REC_HEADS)])


def _mlstm_prompt_kernel(q_ref, k_ref, v_ref, o_ref, gate_ref, hn_ref, hh_ref, c_ref, n_ref, m_ref):
    lc = REC_CHUNK
    dh = LANES
    hds = N_REC_HEADS

    @pl.when(pl.program_id(1) == 0)
    def _():
        c_ref[...] = jnp.zeros_like(c_ref)
        n_ref[...] = jnp.zeros_like(n_ref)
        m_ref[...] = jnp.zeros_like(m_ref)

    ri = _iota((1, lc, lc), 1)
    ci = _iota((1, lc, lc), 2)
    eye = ri == ci
    tril = ci <= ri
    q3 = q_ref[...]
    k3 = k_ref[...] * (dh ** -0.5)
    v3 = v_ref[...]
    g = gate_ref[...]
    i_col = _head_columns(g, 0)
    lf_col = _log_sigmoid(_head_columns(g, hds))
    c_old = c_ref[...]
    n_old = n_ref[...]
    m_old = m_ref[...]
    qb, kb, vb = q3.astype(BF16), k3.astype(BF16), v3.astype(BF16)
    cb = c_old.astype(BF16)

    lf_row = jnp.sum(jnp.where(eye, lf_col, 0.0), axis=1, keepdims=True)
    i_row = jnp.sum(jnp.where(eye, i_col, 0.0), axis=1, keepdims=True)
    b_col = jnp.sum(jnp.where(tril, lf_row, 0.0), axis=2, keepdims=True)
    b_row = jnp.sum(jnp.where(ri <= ci, lf_col, 0.0), axis=1, keepdims=True)
    a_row = i_row - b_row
    a_col = i_col - b_col
    cm_col = jnp.max(jnp.where(tril, a_row, -jnp.inf), axis=2, keepdims=True)
    mt_col = b_col + jnp.maximum(m_old, cm_col)
    inter = jnp.exp(m_old + b_col - mt_col)
    dmat = jnp.where(tril, jnp.exp(jnp.where(tril, a_row + (b_col - mt_col), 0.0)), 0.0)
    s = dmat * jnp.stack([_dot_nt(qb[h], kb[h]) for h in range(hds)])
    sb = s.astype(BF16)
    num = inter * jnp.stack([_dot(qb[h], cb[h]) for h in range(hds)]) \
        + jnp.stack([_dot(sb[h], vb[h]) for h in range(hds)])
    den = inter * jnp.sum(q3 * n_old, axis=2, keepdims=True) + jnp.sum(s, axis=2, keepdims=True)
    hh = num / jnp.maximum(jnp.abs(den), jnp.exp(-mt_col))
    b_last = b_col[:, lc - 1:lc, :]
    m_new = mt_col[:, lc - 1:lc, :]
    w_last = jnp.exp(a_col + (b_last - m_new))
    decay = jnp.exp(m_old + b_last - m_new)
    kw = k3 * w_last
    kwb = kw.astype(BF16)
    c_ref[...] = decay * c_old + jnp.stack([_dot_tn(kwb[h], vb[h]) for h in range(hds)])
    n_ref[...] = decay * n_old + jnp.sum(kw, axis=1, keepdims=True)
    m_ref[...] = m_new

    y = hh * lax.rsqrt(jnp.mean(hh * hh, axis=2, keepdims=True) + EPS) * hn_ref[...]
    out = (y * _sigmoid(o_ref[...])).astype(hh_ref.dtype)
    for h in range(hds):
        hh_ref[:, h * dh:(h + 1) * dh] = out[h]


def _mlstm_prompt(proj, gates, hn_g, batch, seq):
    lc = REC_CHUNK
    nc = seq // lc
    hds, dh = N_REC_HEADS, LANES
    d = hds * dh

    def part(kind):
        return pl.BlockSpec((hds, lc, dh), lambda b, c: (kind, b * nc + c, 0))

    return pl.pallas_call(
        _mlstm_prompt_kernel,
        grid=(batch, nc),
        in_specs=[
            part(0), part(1), part(2), part(3),
            pl.BlockSpec((lc, LANES), lambda b, c: (b * nc + c, 0)),
            pl.BlockSpec((hds, 1, dh), lambda b, c: (0, 0, 0)),
        ],
        out_specs=[
            pl.BlockSpec((lc, d), lambda b, c: (b * nc + c, 0)),
            pl.BlockSpec((None, hds, dh, dh), lambda b, c: (b, 0, 0, 0)),
            pl.BlockSpec((None, hds, 1, dh), lambda b, c: (b, 0, 0, 0)),
            pl.BlockSpec((None, hds, 1, 1), lambda b, c: (b, 0, 0, 0)),
        ],
        out_shape=[
            jax.ShapeDtypeStruct((batch * seq, d), BF16),
            jax.ShapeDtypeStruct((batch, hds, dh, dh), F32),
            jax.ShapeDtypeStruct((batch, hds, 1, dh), F32),
            jax.ShapeDtypeStruct((batch, hds, 1, 1), F32),
        ],
        compiler_params=_cp("arbitrary", "arbitrary"),
        name="mlstm_prompt",
    )(proj, proj, proj, proj, gates, hn_g.reshape(hds, 1, dh))


REC_STEP_BATCH = 8


def _mlstm_step_kernel(p_ref, gate_ref, hn_ref, c0_ref, n0_ref, m0_ref, *rest, n_prev):
    if n_prev:
        cprev_ref, hh_ref, c_ref, n_ref, m_ref = rest
        c_ref[:n_prev] = cprev_ref[...]
    else:
        hh_ref, c_ref, n_ref, m_ref = rest
    dh = LANES
    hds = N_REC_HEADS
    eye = _iota((1, dh, dh), 1) == _iota((1, dh, dh), 2)
    hn = hn_ref[...]
    for bb in range(REC_STEP_BATCH):
        row = slice(bb, bb + 1)
        q3 = p_ref[0:hds, row, :]
        k3 = p_ref[hds:2 * hds, row, :] * (dh ** -0.5)
        v3 = p_ref[2 * hds:3 * hds, row, :]
        o3 = p_ref[3 * hds:4 * hds, row, :]
        g = gate_ref[row, :]
        i_pre = _head_columns(g, 0)
        lf = _log_sigmoid(_head_columns(g, hds))
        m_old = m0_ref[bb]
        n_old = n0_ref[bb]
        c_old = c0_ref[bb]
        mt = lf + jnp.maximum(m_old, i_pre - lf)
        inter = jnp.exp(m_old + lf - mt)
        w_last = jnp.exp(i_pre - mt)
        q_col = jnp.sum(jnp.where(eye, q3, 0.0), axis=2, keepdims=True)
        k_col = jnp.sum(jnp.where(eye, k3, 0.0), axis=2, keepdims=True)
        qc = jnp.sum(q_col * c_old, axis=1, keepdims=True)
        qk = jnp.sum(q3 * k3, axis=2, keepdims=True)
        qn = jnp.sum(q3 * n_old, axis=2, keepdims=True)
        s = w_last * qk
        num = inter * qc + s * v3
        den = inter * qn + s
        hh = num / jnp.maximum(jnp.abs(den), jnp.exp(-mt))
        c_ref[n_prev, bb] = inter * c_old + (w_last * k_col) * v3
        n_ref[bb] = inter * n_old + w_last * k3
        m_ref[bb] = mt
        y = hh * lax.rsqrt(jnp.mean(hh * hh, axis=2, keepdims=True) + EPS) * hn
        out = (y * _sigmoid(o3)).astype(hh_ref.dtype)
        for h in range(hds):
            hh_ref[row, h * dh:(h + 1) * dh] = out[h]


def _mlstm_step(proj, gates, hn_g, state_c, state_n, state_m, j_layer, c_prev):
    nb = proj.shape[1]
    hds, dh = N_REC_HEADS, LANES
    d = hds * dh
    bb = REC_STEP_BATCH
    n_prev = j_layer
    in_specs = [
        pl.BlockSpec((4 * hds, bb, dh), lambda i: (0, i, 0)),
        pl.BlockSpec((bb, LANES), lambda i: (i, 0)),
        pl.BlockSpec((hds, 1, dh), lambda i: (0, 0, 0)),
        pl.BlockSpec((None, bb, hds, dh, dh), lambda i: (j_layer, i, 0, 0, 0)),
        pl.BlockSpec((None, bb, hds, 1, dh), lambda i: (j_layer, i, 0, 0, 0)),
        pl.BlockSpec((None, bb, hds, 1, 1), lambda i: (j_layer, i, 0, 0, 0)),
    ]
    args = [proj, gates, hn_g.reshape(hds, 1, dh), state_c, state_n, state_m]
    if n_prev:
        in_specs.append(pl.BlockSpec((n_prev, bb, hds, dh, dh), lambda i: (0, i, 0, 0, 0)))
        args.append(c_prev)
    return pl.pallas_call(
        functools.partial(_mlstm_step_kernel, n_prev=n_prev),
        grid=(nb // bb,),
        in_specs=in_specs,
        out_specs=[
            pl.BlockSpec((bb, d), lambda i: (i, 0)),
            pl.BlockSpec((n_prev + 1, bb, hds, dh, dh), lambda i: (0, i, 0, 0, 0)),
            pl.BlockSpec((bb, hds, 1, dh), lambda i: (i, 0, 0, 0)),
            pl.BlockSpec((bb, hds, 1, 1), lambda i: (i, 0, 0, 0)),
        ],
        out_shape=[
            jax.ShapeDtypeStruct((nb, d), BF16),
            jax.ShapeDtypeStruct((n_prev + 1, nb, hds, dh, dh), F32),
            jax.ShapeDtypeStruct((nb, hds, 1, dh), F32),
            jax.ShapeDtypeStruct((nb, hds, 1, 1), F32),
        ],
        compiler_params=_cp("arbitrary"),
        name="mlstm_step",
    )(*args)


PAGES_PER_STEP = 16


def _spread_heads(x):
    h, l = x.shape
    return jnp.broadcast_to(x[:, None, :], (h, HEAD_DIM, l)).reshape(h * HEAD_DIM, l)


def _row_to_column(r):
    eye = _iota((LANES, LANES), 0) == _iota((LANES, LANES), 1)
    return jnp.sum(jnp.where(eye, r, 0.0), axis=1, keepdims=True)


def _column_to_row(c):
    eye = _iota((LANES, LANES), 0) == _iota((LANES, LANES), 1)
    return jnp.sum(jnp.where(eye, c, 0.0), axis=0, keepdims=True)


def _decode_attn_kernel(pt_ref, q_ref, kn_ref, vn_ref, slope_ref, u_ref, *refs, page, n_pages, nblk, pp):
    k_refs, v_refs = refs[:pp], refs[pp:2 * pp]
    o_ref = refs[2 * pp]
    q_bc, sb_acc, sb_carry, mb_m, mb_l, mb_acc, mb_gate = refs[2 * pp + 1:]
    g = pl.program_id(1)
    nsteps = n_pages // pp
    hm = MOBA_COLS
    hd = N_ATT_HEADS * HEAD_DIM
    scale = HEAD_DIM ** -0.5
    pos = n_pages * page
    pages_per_blk = MOBA_BLOCK // page
    blks_per_step = pp // pages_per_blk
    slope = slope_ref[...]
    lane_pos = _iota((1, page), 1)

    @pl.when(g == 0)
    def _():
        for c in range(hd // LANES):
            sl = slice(c * LANES, (c + 1) * LANES)
            q_bc[sl, :] = jnp.broadcast_to(_row_to_column(q_ref[:, sl]), (LANES, page))
        sb_acc[...] = jnp.zeros_like(sb_acc)
        sb_carry[...] = jnp.zeros_like(sb_carry)

    q_all = q_bc[...]
    raws, zs = [], []
    for i in range(pp):
        sc = jnp.sum((k_refs[i][...] * q_all).reshape(N_ATT_HEADS, HEAD_DIM, page), axis=1)
        raws.append(sc[:N_MOBA_HEADS])
        zs.append(sc[N_MOBA_HEADS:] * scale)

    tails = [_softplus_tail(z) for z in zs]
    lks = [-jnp.maximum(z, 0.0) - t for z, t in zip(zs, tails)]
    later_all = _dot(_hilo(jnp.concatenate(lks, axis=0)), u_ref[...])
    carry = sb_carry[:, 0:1]
    acc_sb = sb_acc[...]
    for i in range(pp):
        later = later_all[i * N_SB_HEADS:(i + 1) * N_SB_HEADS] + carry
        w = jnp.exp(jnp.minimum(zs[i], 0.0) - tails[i] + later)
        acc_sb = acc_sb + v_refs[i][hm:, :] * _spread_heads(w)
        carry = carry + jnp.sum(lks[i], axis=-1, keepdims=True)
    sb_acc[...] = acc_sb
    sb_carry[...] = jnp.broadcast_to(carry, (N_SB_HEADS, LANES))

    for half in range(blks_per_step):
        idx = [half * pages_per_blk + t for t in range(pages_per_blk)]
        s_parts = []
        for i in idx:
            pg = n_pages - 1 - (g * pp + i)
            dist = (pos - (pg * page + lane_pos)).astype(F32)
            s_parts.append(raws[i] * scale - slope * dist)
        gate_blk = functools.reduce(lambda a, b: a + b, [jnp.sum(raws[i], axis=-1, keepdims=True) for i in idx])
        m_blk = functools.reduce(jnp.maximum, [jnp.max(s, axis=-1, keepdims=True) for s in s_parts])
        l_blk = jnp.zeros((N_MOBA_HEADS, 1), F32)
        a_blk = jnp.zeros((hm, page), F32)
        for i, s in zip(idx, s_parts):
            p = jnp.exp(s - m_blk)
            l_blk = l_blk + jnp.sum(p, axis=-1, keepdims=True)
            a_blk = a_blk + v_refs[i][:hm, :] * _spread_heads(p)
        n_blk = nblk - 1 - (g * blks_per_step + half)
        mb_m[n_blk] = jnp.broadcast_to(m_blk, (N_MOBA_HEADS, LANES))
        mb_l[n_blk] = jnp.broadcast_to(l_blk, (N_MOBA_HEADS, LANES))
        mb_acc[n_blk] = a_blk
        mb_gate[n_blk] = jnp.broadcast_to(gate_blk * (1.0 / MOBA_BLOCK), (N_MOBA_HEADS, LANES))

    @pl.when(g == nsteps - 1)
    def _():
        gates = [mb_gate[n][:, 0:1] for n in range(nblk)]
        diag = _iota((N_MOBA_HEADS, hm), 0) == _iota((N_MOBA_HEADS, hm), 1) // HEAD_DIM
        q_mb = jnp.where(diag, q_ref[:, :hm], 0.0)
        s_self = jnp.sum(q_mb * kn_ref[:, :hm], axis=-1, keepdims=True) * scale
        sels, m_fin = [], s_self
        for n in range(nblk):
            rank = jnp.zeros((N_MOBA_HEADS, 1), F32)
            for mth in range(nblk):
                if mth == n:
                    continue
                beats = gates[mth] > gates[n]
                if mth < n:
                    beats = beats | (gates[mth] == gates[n])
                rank = rank + beats.astype(F32)
            sel = rank < MOBA_TOPK
            sels.append(sel)
            m_fin = jnp.maximum(m_fin, jnp.where(sel, mb_m[n][:, 0:1], NEG))
        p_self = jnp.exp(s_self - m_fin)
        l_fin = p_self
        a_fin = jnp.zeros((hm, page), F32)
        for n in range(nblk):
            wgt = jnp.where(sels[n], jnp.exp(mb_m[n][:, 0:1] - m_fin), 0.0)
            l_fin = l_fin + wgt * mb_l[n][:, 0:1]
            a_fin = a_fin + _spread_heads(wgt) * mb_acc[n]
        col_mb = jnp.sum(a_fin, axis=-1, keepdims=True)
        col_sb = jnp.sum(acc_sb, axis=-1, keepdims=True)
        w_self = _spread_heads(p_self)
        inv_l = _spread_heads(1.0 / l_fin)
        for c in range(hm // LANES):
            sl = slice(c * LANES, (c + 1) * LANES)
            v_col = _row_to_column(vn_ref[:, sl])
            o_ref[:, sl] = _column_to_row((col_mb[sl] + w_self[sl] * v_col) * inv_l[sl]).astype(o_ref.dtype)
            o_ref[:, hm + c * LANES:hm + (c + 1) * LANES] = _column_to_row(col_sb[sl]).astype(o_ref.dtype)


def _decode_attn(q, k_new, v_new, cache_kt, cache_vt, page_table, j_layer):
    nb = q.shape[0]
    n_pages = page_table.shape[1]
    hd, page = cache_kt.shape[2], cache_kt.shape[3]
    hm = MOBA_COLS
    pp = min(PAGES_PER_STEP, n_pages)
    nblk = n_pages * page // MOBA_BLOCK

    def page_spec(i):
        return pl.BlockSpec((None, None, hd, page),
                            lambda b, g, pt: (j_layer, pt[b * n_pages + (n_pages - 1 - (g * pp + i))], 0, 0))

    row = pl.BlockSpec((None, 1, hd), lambda b, g, pt: (b, 0, 0))
    grid_spec = pltpu.PrefetchScalarGridSpec(
        num_scalar_prefetch=1,
        grid=(nb, n_pages // pp),
        in_specs=[row, row, row,
                  pl.BlockSpec((N_MOBA_HEADS, 1), lambda b, g, pt: (0, 0)),
                  pl.BlockSpec((2 * page, page), lambda b, g, pt: (0, 0))]
                 + [page_spec(i) for i in range(pp)] + [page_spec(i) for i in range(pp)],
        out_specs=pl.BlockSpec((None, 1, hd), lambda b, g, pt: (b, 0, 0)),
        scratch_shapes=[
            pltpu.VMEM((hd, page), F32),
            pltpu.VMEM((hd - hm, page), F32),
            pltpu.VMEM((N_SB_HEADS, LANES), F32),
            pltpu.VMEM((nblk, N_MOBA_HEADS, LANES), F32),
            pltpu.VMEM((nblk, N_MOBA_HEADS, LANES), F32),
            pltpu.VMEM((nblk, hm, page), F32),
            pltpu.VMEM((nblk, N_MOBA_HEADS, LANES), F32),
        ],
    )
    out = pl.pallas_call(
        functools.partial(_decode_attn_kernel, page=page, n_pages=n_pages, nblk=nblk, pp=pp),
        grid_spec=grid_spec,
        out_shape=jax.ShapeDtypeStruct((nb, 1, hd), BF16),
        compiler_params=_cp("arbitrary", "arbitrary"),
        name="decode_attn",
    )(page_table.reshape(-1), q.reshape(nb, 1, hd), k_new.reshape(nb, 1, hd), v_new.reshape(nb, 1, hd),
      jnp.asarray(_alibi_slopes(), F32).reshape(N_MOBA_HEADS, 1), _later_matrix(page),
      *([cache_kt] * pp), *([cache_vt] * pp))
    return out.reshape(nb, hd)


ROW_TILE = 512
QKV_ROW_TILE = 256


def _decoder(x, mod_arr, rows_per_batch, w, cache=None, state=None):
    m, d = x.shape
    depth = w["mlp_w1"].shape[0]
    batch = m // rows_per_batch
    tm = m if rows_per_batch == 1 else min(ROW_TILE, rows_per_batch)
    kv, c_stack, cs, ns, ms = None, None, [], [], []
    for l in range(depth):
        mod = _Mod(mod_arr, l, d, tm, rows_per_batch)
        j = l // 2
        if l % 2 == 0:
            if cache is None:
                mod_qkv = _Mod(mod_arr, l, d, min(QKV_ROW_TILE, rows_per_batch), rows_per_batch)
                q, kt, vt = _qkv_proj(x, w["norm1_g"], mod_qkv, w, j, rows_per_batch, False, kv)
                o = _prompt_attn(q, kt, vt, j, batch, rows_per_batch)
            else:
                q, kt, vt, k, v = _qkv_proj(x, w["norm1_g"], mod, w, j, rows_per_batch, True, kv)
                o = _decode_attn(q, k, v, cache[0], cache[1], cache[2], j)
            kv = (kt, vt)
            x = _res_matmul(o, w["att_w_out"], j, x, mod, 2)
        else:
            proj, gates = _rec_in_proj(x, w["norm1_g"], mod, w["rec_w_main"], w["rec_w_gate"], w["rec_b_gate"], j)
            if state is None:
                hh, c_new, n_new, m_new = _mlstm_prompt(proj, gates, w["rec_hn_g"][j], batch, rows_per_batch)
                cs.append(c_new)
            else:
                hh, c_stack, n_new, m_new = _mlstm_step(proj, gates, w["rec_hn_g"][j], state[0], state[1], state[2],
                                                        j, c_stack)
            ns.append(n_new.reshape(batch, N_REC_HEADS, LANES))
            ms.append(m_new.reshape(batch, N_REC_HEADS))
            x = _res_matmul(hh, w["rec_w_out"], j, x, mod, 2)
        hid = _nm_matmul(x, w["norm2_g"], mod, (4, 3), w["mlp_w1"], l, relu2=True, out_dtype=BF16)
        x = _res_matmul(hid, w["mlp_w2"], l, x, mod, 5)
    c_all = jnp.stack(cs) if state is None else c_stack
    return x, kv[0], kv[1], c_all, jnp.stack(ns), jnp.stack(ms)


def _rows_from_feature_major(t):
    na, b, _, s = t.shape
    return jnp.transpose(t.reshape(na, b, N_ATT_HEADS, HEAD_DIM, s), (0, 1, 4, 2, 3))


def kernel(x_prompt, x_sample, cache_k, cache_v, page_table, state_C, state_n, state_m, c_prompt, c_sample,
           ada_w, ada_b, norm1_g, norm2_g, att_w_in, att_qn_g, att_kn_g, att_w_out,
           rec_w_in, rec_b_i, rec_b_f, rec_hn_g, rec_w_out, mlp_w1, mlp_w2):
    bp, seq, d = x_prompt.shape
    bs = x_sample.shape[0]
    depth = ada_w.shape[0]
    hds = N_REC_HEADS
    n_rec = rec_w_in.shape[0]
    hd = N_ATT_HEADS * HEAD_DIM

    mod = _adaln_mod(jnp.concatenate([c_prompt, c_sample], axis=0), ada_w, ada_b)
    mod_p = mod[:, :bp].reshape(depth, bp, 1, 6 * d)
    mod_s = mod[:, bp:]

    gate_w = jnp.pad(rec_w_in[:, :, 4 * d:], ((0, 0), (0, 0), (0, LANES - 2 * hds)))
    gate_b = jnp.pad(jnp.concatenate([rec_b_i, rec_b_f], axis=1), ((0, 0), (0, LANES - 2 * hds)))
    w_in16 = att_w_in.astype(BF16)
    w = {
        "norm1_g": norm1_g.reshape(depth, 1, d),
        "norm2_g": norm2_g.reshape(depth, 1, d),
        "att_wq": w_in16[:, :, :hd],
        "att_wk": w_in16[:, :, hd:2 * hd],
        "att_wv": w_in16[:, :, 2 * hd:],
        "att_wkt": jnp.transpose(w_in16[:, :, hd:2 * hd], (0, 2, 1)),
        "att_wvt": jnp.transpose(w_in16[:, :, 2 * hd:], (0, 2, 1)),
        "att_qn_g": jnp.tile(att_qn_g, (1, LANES // HEAD_DIM)).reshape(-1, 1, LANES),
        "att_kn_g": jnp.tile(att_kn_g, (1, LANES // HEAD_DIM)).reshape(-1, 1, LANES),
        "att_kn_gc": att_kn_g.reshape(-1, HEAD_DIM, 1),
        "att_w_out": att_w_out.astype(BF16),
        "rec_w_main": rec_w_in[:, :, :4 * d].astype(BF16),
        "rec_w_gate": gate_w.astype(BF16),
        "rec_b_gate": gate_b.reshape(n_rec, 1, LANES),
        "rec_hn_g": rec_hn_g,
        "rec_w_out": rec_w_out.astype(BF16),
        "mlp_w1": mlp_w1.astype(BF16),
        "mlp_w2": mlp_w2.astype(BF16),
    }

    yp, ktp, vtp, cp_, np_, mp = _decoder(x_prompt.reshape(bp * seq, d), mod_p, seq, w)

    def feature_major_cache(c):
        nl, pool, page = c.shape[:3]
        return jnp.transpose(c, (0, 1, 3, 4, 2)).reshape(nl, pool, hd, page)

    ys, kts, vts, csm, nsm, msm = _decoder(x_sample.reshape(bs, d), mod_s, 1, w,
                                           cache=(feature_major_cache(cache_k), feature_major_cache(cache_v),
                                                  page_table),
                                           state=(state_C, state_n.reshape(n_rec, bs, hds, 1, LANES),
                                                  state_m.reshape(n_rec, bs, hds, 1, 1)))
    k_s = jnp.transpose(_rows_from_feature_major(kts), (0, 2, 1, 3, 4))
    v_s = jnp.transpose(_rows_from_feature_major(vts), (0, 2, 1, 3, 4))
    return (yp.reshape(bp, seq, d), ys.reshape(bs, 1, d),
            _rows_from_feature_major(ktp), _rows_from_feature_major(vtp), k_s, v_s,
            cp_, np_, mp, csm, nsm, msm)
```
